```python
import jax
import jax.numpy as jnp
from jax import lax
import numpy as np

D_MODEL = 1024
BATCH = 2
SEQ = 8192
DEPTH = 2

CTX_LEN = 256
GRID_W = 64
BLOCK = 128
WINDOW = 128
ROPE_BASE = 10000.0
EPS = 1e-6
NEG_INF = -1e30

H_A = 8
HKV_A = 2
G_A = H_A // HKV_A
HD_A = 64
SCALE_A = HD_A ** -0.5

H_B = 8
Q_LORA = 384
KV_LORA = 256
NOPE_B = 64
ROPE_B = 32
V_B = 64
SCALE_B = (NOPE_B + ROPE_B) ** -0.5

W_C = 512
CONV_K = 3

W_BRANCH = 512

IN_SIZES = (H_A * HD_A, HKV_A * HD_A, HKV_A * HD_A, W_BRANCH,
            Q_LORA, KV_LORA, ROPE_B, W_BRANCH,
            W_C, W_C, W_C, W_BRANCH,
            D_MODEL, D_MODEL, D_MODEL)
D_IN = sum(IN_SIZES)

kernel_name = "hybrid_gated_branch_diffusion_trunk"


def rms_norm(x, g):
    xf = x.astype(jnp.float32)
    y = xf * lax.rsqrt(jnp.mean(xf * xf, axis=-1, keepdims=True) + EPS)
    return (y * g.astype(jnp.float32)).astype(x.dtype)


def axial_rope_angles(rows, rot_dim):
    row = jnp.repeat(jnp.arange(rows, dtype=jnp.float32), GRID_W)
    col = jnp.tile(jnp.arange(GRID_W, dtype=jnp.float32), rows)
    axis_dim = rot_dim // 2
    inv = ROPE_BASE ** (-jnp.arange(0, axis_dim, 2, dtype=jnp.float32) / axis_dim)
    ang = jnp.concatenate([row[:, None] * inv, col[:, None] * inv], axis=-1)
    return jnp.cos(ang), jnp.sin(ang)


def apply_rope(x, cos, sin):
    xf = x.astype(jnp.float32)
    x1, x2 = jnp.split(xf, 2, axis=-1)
    c = cos[None, :, None, :]
    s = sin[None, :, None, :]
    return jnp.concatenate([x1 * c - x2 * s, x2 * c + x1 * s], axis=-1).astype(x.dtype)


def in_proj(h, w_in):
    splits = np.cumsum(IN_SIZES)[:-1].tolist()
    return jnp.split(h @ w_in, splits, axis=-1)


def window_attn_latent(q, k, v, k_ctx, v_ctx, sink):
    B, S = q.shape[:2]
    nb = S // BLOCK
    qb = q.reshape(B, nb, BLOCK, HKV_A, G_A, HD_A)
    pad = ((0, 0), (BLOCK, BLOCK), (0, 0), (0, 0))
    kp = jnp.pad(k, pad).reshape(B, nb + 2, BLOCK, HKV_A, HD_A)
    vp = jnp.pad(v, pad).reshape(B, nb + 2, BLOCK, HKV_A, HD_A)
    kw = jnp.concatenate([kp[:, :-2], kp[:, 1:-1], kp[:, 2:]], axis=2)
    vw = jnp.concatenate([vp[:, :-2], vp[:, 1:-1], vp[:, 2:]], axis=2)
    s_loc = jnp.einsum('bnqhgd,bnkhd->bnhgqk', qb, kw).astype(jnp.float32) * SCALE_A
    qi = jnp.arange(BLOCK)[:, None] + BLOCK
    kj = jnp.arange(3 * BLOCK)[None, :]
    band = jnp.abs(kj - qi) <= WINDOW
    kabs = jnp.arange(nb)[:, None] * BLOCK + jnp.arange(3 * BLOCK)[None, :] - BLOCK
    valid = (kabs >= 0) & (kabs < S)
    mask = band[None, :, :] & valid[:, None, :]
    s_loc = jnp.where(mask[None, :, None, None], s_loc, NEG_INF)
    s_ctx = jnp.einsum('bnqhgd,blhd->bnhgql', qb, k_ctx).astype(jnp.float32) * SCALE_A
    s_sink = jnp.broadcast_to(sink.astype(jnp.float32).reshape(1, 1, HKV_A, G_A, 1, 1),
                              s_loc.shape[:-1] + (1,))
    p = jax.nn.softmax(jnp.concatenate([s_loc, s_ctx, s_sink], axis=-1), axis=-1)
    p_loc = p[..., :3 * BLOCK].astype(v.dtype)
    p_ctx = p[..., 3 * BLOCK:3 * BLOCK + k_ctx.shape[1]].astype(v.dtype)
    o = (jnp.einsum('bnhgqk,bnkhd->bnqhgd', p_loc, vw)
         + jnp.einsum('bnhgql,blhd->bnqhgd', p_ctx, v_ctx))
    return o.reshape(B, S, H_A * HD_A)


def ctx_gqa_attn(q, k, v, sink):
    B, L = q.shape[:2]
    qg = q.reshape(B, L, HKV_A, G_A, HD_A)
    s = jnp.einsum('blhgd,bmhd->bhglm', qg, k).astype(jnp.float32) * SCALE_A
    sk = jnp.broadcast_to(sink.astype(jnp.float32).reshape(1, HKV_A, G_A, 1, 1), (B, HKV_A, G_A, L, 1))
    p = jax.nn.softmax(jnp.concatenate([s, sk], axis=-1), axis=-1)[..., :L]
    o = jnp.einsum('bhglm,bmhd->blhgd', p.astype(v.dtype), v)
    return o.reshape(B, L, H_A * HD_A)


def mla_project(q_lat, kv_lat, g_qa, w_qb, g_kva, w_kvb):
    B, T = q_lat.shape[:2]
    q = (rms_norm(q_lat, g_qa) @ w_qb).reshape(B, T, H_B, NOPE_B + ROPE_B)
    kv = (rms_norm(kv_lat, g_kva) @ w_kvb).reshape(B, T, H_B, NOPE_B + V_B)
    return q[..., :NOPE_B], q[..., NOPE_B:], kv[..., :NOPE_B], kv[..., NOPE_B:]


def mla_attn_latent(qn, qr, kn, kr, v, kn_c, kr_c, v_c):
    B, S = qn.shape[:2]
    nb = S // BLOCK
    keys_n = jnp.concatenate([kn, kn_c], axis=1)
    keys_r = jnp.concatenate([kr, kr_c], axis=1)
    vals = jnp.concatenate([v, v_c], axis=1)

    def block(qs):
        q_n, q_r = qs
        s = (jnp.einsum('bqhd,bkhd->bhqk', q_n, keys_n)
             + jnp.einsum('bqhd,bkd->bhqk', q_r, keys_r))
        p = jax.nn.softmax(s.astype(jnp.float32) * SCALE_B, axis=-1)
        return jnp.einsum('bhqk,bkhd->bqhd', p.astype(vals.dtype), vals)

    qn_b = qn.reshape(B, nb, BLOCK, H_B, NOPE_B).swapaxes(0, 1)
    qr_b = qr.reshape(B, nb, BLOCK, H_B, ROPE_B).swapaxes(0, 1)
    o = lax.map(block, (qn_b, qr_b))
    return o.swapaxes(0, 1).reshape(B, S, H_B * V_B)


def ctx_mla_attn(qn, qr, kn, kr, v):
    B, L = qn.shape[:2]
    s = jnp.einsum('blhd,bmhd->bhlm', qn, kn) + jnp.einsum('blhd,bmd->bhlm', qr, kr)
    p = jax.nn.softmax(s.astype(jnp.float32) * SCALE_B, axis=-1)
    o = jnp.einsum('bhlm,bmhd->blhd', p.astype(v.dtype), v)
    return o.reshape(B, L, H_B * V_B)


def short_gated_conv(b_gate, c_gate, u, conv_w):
    z = jnp.pad(c_gate * u, ((0, 0), (1, 1), (0, 0)))
    y = conv_w[0] * z[:, :-2] + conv_w[1] * z[:, 1:-1] + conv_w[2] * z[:, 2:]
    return b_gate * y


def merge_branches(ya, yb, yc, za, zb, zc, ga, gb, gc, w_branch, w_o):
    m = (jax.nn.sigmoid(ga) * ((ya * jax.nn.silu(za)) @ w_branch[0])
         + jax.nn.sigmoid(gb) * ((yb * jax.nn.silu(zb)) @ w_branch[1])
         + jax.nn.sigmoid(gc) * ((yc * jax.nn.silu(zc)) @ w_branch[2]))
    return m @ w_o


def mixer_layer(x, ctx, mod_x, mod_c, rope_a, rope_b, g_pre, g_post, w_in, sink,
                g_qa, w_qb, g_kva, w_kvb, conv_w, w_branch, w_o, update_ctx):
    B, S, _ = x.shape
    L = ctx.shape[1]
    shift_x, scale_x, gate_x = [m[:, None, :] for m in jnp.split(mod_x, 3, axis=-1)]
    shift_c, scale_c, gate_c = jnp.split(mod_c, 3, axis=-1)
    hx = rms_norm(x, g_pre) * (1.0 + scale_x) + shift_x
    hc = rms_norm(ctx, g_pre) * (1.0 + scale_c) + shift_c
    (qa_x, ka_x, va_x, za_x, qlb_x, kvlb_x, krb_x, zb_x,
     bc_x, cc_x, uc_x, zc_x, ga_x, gb_x, gc_x) = in_proj(hx, w_in)
    (qa_c, ka_c, va_c, za_c, qlb_c, kvlb_c, krb_c, zb_c,
     bc_c, cc_c, uc_c, zc_c, ga_c, gb_c, gc_c) = in_proj(hc, w_in)

    cos_a, sin_a = rope_a
    qa = apply_rope(qa_x.reshape(B, S, H_A, HD_A), cos_a, sin_a)
    ka = apply_rope(ka_x.reshape(B, S, HKV_A, HD_A), cos_a, sin_a)
    va = va_x.reshape(B, S, HKV_A, HD_A)
    kac = ka_c.reshape(B, L, HKV_A, HD_A)
    vac = va_c.reshape(B, L, HKV_A, HD_A)
    ya_x = window_attn_latent(qa, ka, va, kac, vac, sink)

    cos_b, sin_b = rope_b
    qn_x, qr_x, kn_x, vb_x = mla_project(qlb_x, kvlb_x, g_qa, w_qb, g_kva, w_kvb)
    qr_x = apply_rope(qr_x, cos_b, sin_b)
    kr_x = apply_rope(krb_x[:, :, None, :], cos_b, sin_b)[:, :, 0]
    qn_c, qr_c, kn_c, vb_c = mla_project(qlb_c, kvlb_c, g_qa, w_qb, g_kva, w_kvb)
    yb_x = mla_attn_latent(qn_x, qr_x, kn_x, kr_x, vb_x, kn_c, krb_c, vb_c)

    yc_x = short_gated_conv(bc_x, cc_x, uc_x, conv_w)

    out_x = merge_branches(ya_x, yb_x, yc_x, za_x, zb_x, zc_x, ga_x, gb_x, gc_x, w_branch, w_o)
    x = x + gate_x * rms_norm(out_x, g_post)

    if update_ctx:
        ya_c = ctx_gqa_attn(qa_c.reshape(B, L, H_A, HD_A), kac, vac, sink)
        yb_c = ctx_mla_attn(qn_c, qr_c, kn_c, krb_c, vb_c)
        yc_c = short_gated_conv(bc_c, cc_c, uc_c, conv_w)
        out_c = merge_branches(ya_c, yb_c, yc_c, za_c, zb_c, zc_c, ga_c, gb_c, gc_c, w_branch, w_o)
        ctx = ctx + gate_c * rms_norm(out_c, g_post)
    return x, ctx


def setup_inputs(seed: int = 0) -> dict:
    key = jax.random.key(seed)
    ks = jax.random.split(key, 17)
    f32 = jnp.float32

    def nrm(k, shape, scale):
        return jax.random.normal(k, shape, f32) * scale

    return {
        "x": nrm(ks[0], (BATCH, SEQ, D_MODEL), 1.0),
        "c": nrm(ks[1], (BATCH, D_MODEL), 1.0),
        "ctx": nrm(ks[2], (BATCH, CTX_LEN, D_MODEL), 1.0),
        "c_ctx": nrm(ks[3], (D_MODEL,), 1.0),
        "w_mod": nrm(ks[4], (DEPTH, D_MODEL, 3 * D_MODEL), 0.5 * D_MODEL ** -0.5),
        "b_mod": nrm(ks[5], (DEPTH, 3 * D_MODEL), 0.01),
        "g_pre": 1.0 + nrm(ks[6], (DEPTH, D_MODEL), 0.05),
        "g_post": 1.0 + nrm(ks[7], (DEPTH, D_MODEL), 0.05),
        "w_in": nrm(ks[8], (DEPTH, D_MODEL, D_IN), D_MODEL ** -0.5),
        "sink": nrm(ks[9], (DEPTH, H_A), 1.0),
        "g_qa": 1.0 + nrm(ks[10], (DEPTH, Q_LORA), 0.05),
        "w_qb": nrm(ks[11], (DEPTH, Q_LORA, H_B * (NOPE_B + ROPE_B)), Q_LORA ** -0.5),
        "g_kva": 1.0 + nrm(ks[12], (DEPTH, KV_LORA), 0.05),
        "w_kvb": nrm(ks[13], (DEPTH, KV_LORA, H_B * (NOPE_B + V_B)), KV_LORA ** -0.5),
        "conv_w": nrm(ks[14], (DEPTH, CONV_K, W_C), CONV_K ** -0.5),
        "w_branch": nrm(ks[15], (DEPTH, 3, W_BRANCH, D_MODEL), W_BRANCH ** -0.5),
        "w_o": nrm(ks[16], (DEPTH, D_MODEL, D_MODEL), D_MODEL ** -0.5),
    }


def reference(x, c, ctx, c_ctx, w_mod, b_mod, g_pre, g_post, w_in, sink,
              g_qa, w_qb, g_kva, w_kvb, conv_w, w_branch, w_o):
    rows = x.shape[1] // GRID_W
    rope_a = axial_rope_angles(rows, HD_A)
    rope_b = axial_rope_angles(rows, ROPE_B)
    sc = jax.nn.silu(c)
    scc = jax.nn.silu(c_ctx)
    for i in range(DEPTH):
        mod_x = sc @ w_mod[i] + b_mod[i]
        mod_c = scc @ w_mod[i] + b_mod[i]
        x, ctx = mixer_layer(x, ctx, mod_x, mod_c, rope_a, rope_b, g_pre[i], g_post[i],
                             w_in[i], sink[i], g_qa[i], w_qb[i], g_kva[i], w_kvb[i],
                             conv_w[i], w_branch[i], w_o[i], update_ctx=(i < DEPTH - 1))
    return x
```

```python
import functools

import jax
import jax.numpy as jnp
import numpy as np
from jax import lax
from jax.experimental import pallas as pl
from jax.experimental.pallas import tpu as pltpu

F32 = jnp.float32
BF16 = jnp.bfloat16

D_MODEL = 1024
SEQ = 8192
DEPTH = 2
CTX_LEN = 256
GRID_W = 64
BLOCK = 128
WINDOW = 128
ROPE_BASE = 10000.0
EPS = 1e-6
NEG_INF = -1e30

H_A = 8
HKV_A = 2
HD_A = 64
SCALE_A = HD_A ** -0.5

H_B = 8
Q_LORA = 384
KV_LORA = 256
NOPE_B = 64
ROPE_B = 32
V_B = 64
SCALE_B = (NOPE_B + ROPE_B) ** -0.5
EXP2_SCALE_B = SCALE_B * float(np.log2(np.e))

W_C = 512
W_BRANCH = 512

IN_SIZES = (H_A * HD_A, HKV_A * HD_A, HKV_A * HD_A, W_BRANCH,
            Q_LORA, KV_LORA, ROPE_B, W_BRANCH,
            W_C, W_C, W_C, W_BRANCH,
            D_MODEL, D_MODEL, D_MODEL)

LANES = 128
HEAD_PAD = 128

C_QA = 0
C_KDUP = 512
C_VDUP = 768
C_ZA = 1024
C_QL = 1536
C_KVL = 1920
C_KR = 2176
C_ZB = 2304
C_BC = 2816
C_CC = 3328
C_UC = 3840
C_ZC = 4352
C_GA = 4864
C_GB = 5888
C_GC = 6912
C_END = 7936

VMEM_LIMIT = 56 * 1024 * 1024

TM_FRONT = 256
TM_MERGE = 256
TQ_MLA = 512
TK_MLA = 512


def _silu(v):
    return v * jax.nn.sigmoid(v)


def _rms(v, g):
    return v * lax.rsqrt(jnp.mean(v * v, axis=-1, keepdims=True) + EPS) * g


def _dot(a, b):
    return jnp.dot(a, b, preferred_element_type=F32)


def _dot_t(a, b):
    return lax.dot_general(a, b, (((1,), (1,)), ((), ())), preferred_element_type=F32)


def _lane_masks(dtype):
    lane = lax.broadcasted_iota(jnp.int32, (1, LANES), 1)
    lo = (lane < LANES // 2).astype(dtype)
    return lo, (1 - lo).astype(dtype)


def _mod_kernel(c_ref, w_ref, b_ref, o_ref):
    sc = _silu(c_ref[...])
    o_ref[0] = jnp.dot(sc, w_ref[0], preferred_element_type=F32,
                       precision=lax.Precision.HIGHEST) + b_ref[0]


def _modulation(c_rows, w_mod, b_mod):
    n_chunk = 3
    return pl.pallas_call(
        _mod_kernel,
        grid=(DEPTH, n_chunk),
        in_specs=[
            pl.BlockSpec((8, D_MODEL), lambda l, j: (0, 0)),
            pl.BlockSpec((1, D_MODEL, D_MODEL), lambda l, j: (l, 0, j)),
            pl.BlockSpec((1, 1, D_MODEL), lambda l, j: (l, 0, j)),
        ],
        out_specs=pl.BlockSpec((1, 8, D_MODEL), lambda l, j: (l, 0, j)),
        out_shape=jax.ShapeDtypeStruct((DEPTH, 8, 3 * D_MODEL), F32),
        compiler_params=pltpu.CompilerParams(
            dimension_semantics=("arbitrary", "arbitrary"), vmem_limit_bytes=VMEM_LIMIT),
        name="modulation",
    )(c_rows, w_mod, b_mod.reshape(DEPTH, 1, 3 * D_MODEL))


def _rope_tile(v, cos, sin, first_half, up_shift, dn_shift):
    rot = jnp.where(first_half, pltpu.roll(v, up_shift, 1), pltpu.roll(v, dn_shift, 1))
    return v * cos + rot * sin


def _front_kernel(x_ref, shift_ref, scale_ref, gpre_ref, w_ref, gqa_ref, wqb_ref, gkva_ref,
                  wkn_ref, wv_ref, *rest, rope):
    if rope:
        cosa_ref, sina_ref, cosb_ref, sinb_ref = rest[:4]
        outs = rest[4:]
    else:
        outs = rest
    (qa_ref, kdup_ref, vdup_ref, sza_ref, qb_ref, kb_ref, vb_ref, szb_ref,
     bc_ref, cu_ref, szc_ref, sga_ref, sgb_ref, sgc_ref) = outs

    x = x_ref[...]
    h = _rms(x, gpre_ref[...]) * (1.0 + scale_ref[0]) + shift_ref[0]
    h = h.astype(BF16)

    def mm(a, b):
        return _dot(h, w_ref[:, a:b])

    if rope:
        lane = lax.broadcasted_iota(jnp.int32, (1, LANES), 1)
        first_a = (lane % HD_A) < (HD_A // 2)
        first_b = lane < (NOPE_B + ROPE_B // 2)
        cosa, sina = cosa_ref[...], sina_ref[...]
        cosb, sinb = cosb_ref[...], sinb_ref[...]

        def rope_a(v):
            return _rope_tile(v, cosa, sina, first_a, LANES - HD_A // 2, HD_A // 2)

        def rope_b(v):
            return _rope_tile(v, cosb, sinb, first_b, LANES - ROPE_B // 2, ROPE_B // 2)
    else:
        rope_a = rope_b = lambda v: v

    qa = mm(C_QA, C_KDUP)
    for c in range(4):
        qa_ref[:, c * LANES:(c + 1) * LANES] = rope_a(qa[:, c * LANES:(c + 1) * LANES]).astype(BF16)
    kd = mm(C_KDUP, C_VDUP)
    for c in range(2):
        kdup_ref[:, c * LANES:(c + 1) * LANES] = rope_a(kd[:, c * LANES:(c + 1) * LANES]).astype(BF16)
    vdup_ref[...] = mm(C_VDUP, C_ZA).astype(BF16)
    sza_ref[...] = _silu(mm(C_ZA, C_QL)).astype(BF16)

    qn = _rms(mm(C_QL, C_KVL), gqa_ref[...]).astype(BF16)
    qb = _dot(qn, wqb_ref[...])
    for hh in range(H_B):
        sl = slice(hh * HEAD_PAD, (hh + 1) * HEAD_PAD)
        qb_ref[:, sl] = rope_b(qb[:, sl]).astype(BF16)
    kvn = _rms(mm(C_KVL, C_KR), gkva_ref[...]).astype(BF16)
    kn = _dot(kvn, wkn_ref[...])
    kr = rope_b(mm(C_KR, C_ZB))
    for hh in range(H_B):
        sl = slice(hh * HEAD_PAD, (hh + 1) * HEAD_PAD)
        kb_ref[:, sl] = (kn[:, sl] + kr).astype(BF16)
    vb_ref[...] = _dot(kvn, wv_ref[...]).astype(BF16)
    szb_ref[...] = _silu(mm(C_ZB, C_BC)).astype(BF16)

    bc_ref[...] = mm(C_BC, C_CC).astype(BF16)
    cu_ref[...] = (mm(C_CC, C_UC) * mm(C_UC, C_ZC)).astype(BF16)
    szc_ref[...] = _silu(mm(C_ZC, C_GA)).astype(BF16)

    sga_ref[...] = jax.nn.sigmoid(mm(C_GA, C_GB)).astype(BF16)
    sgb_ref[...] = jax.nn.sigmoid(mm(C_GB, C_GC)).astype(BF16)
    sgc_ref[...] = jax.nn.sigmoid(mm(C_GC, C_END)).astype(BF16)


_FRONT_OUT_COLS = (512, 256, 256, 512, 1024, 1024, 512, 512, 512, 512, 512, 1024, 1024, 1024)


def _front(rows, mod3, mod_row_fn, gpre, w2, gqa, wqb, gkva, wkn, wv, tables, tiles_per_seq):
    t_rows = rows.shape[0]
    tm = TM_FRONT
    n_tiles = t_rows // tm
    rope = tables is not None
    const = lambda shape: pl.BlockSpec(shape, lambda i: (0,) * len(shape),
                                       pipeline_mode=pl.Buffered(1))
    in_specs = [
        pl.BlockSpec((tm, D_MODEL), lambda i: (i, 0)),
        pl.BlockSpec((1, 1, D_MODEL), lambda i: (mod_row_fn(i), 0, 0)),
        pl.BlockSpec((1, 1, D_MODEL), lambda i: (mod_row_fn(i), 0, 1)),
        const((1, D_MODEL)),
        const((D_MODEL, C_END)),
        const((1, Q_LORA)),
        const((Q_LORA, H_B * HEAD_PAD)),
        const((1, KV_LORA)),
        const((KV_LORA, H_B * HEAD_PAD)),
        const((KV_LORA, H_B * V_B)),
    ]
    args = [rows, mod3, mod3, gpre, w2, gqa, wqb, gkva, wkn, wv]
    if rope:
        in_specs += [pl.BlockSpec((tm, LANES), lambda i: (i % tiles_per_seq, 0))] * 4
        args += list(tables)
    out_specs = [pl.BlockSpec((tm, c), lambda i: (i, 0)) for c in _FRONT_OUT_COLS]
    out_shape = [jax.ShapeDtypeStruct((t_rows, c), BF16) for c in _FRONT_OUT_COLS]
    return pl.pallas_call(
        functools.partial(_front_kernel, rope=rope),
        grid=(n_tiles,),
        in_specs=in_specs,
        out_specs=out_specs,
        out_shape=out_shape,
        compiler_params=pltpu.CompilerParams(
            dimension_semantics=("arbitrary",), vmem_limit_bytes=VMEM_LIMIT),
        name="front_rope" if rope else "front_ctx",
    )(*args)


def _gqa_head(qm, keys, vals, sink, masks):
    ss = []
    for k, msk in zip(keys, masks):
        s = _dot_t(qm, k) * SCALE_A
        if msk is not None:
            s = jnp.where(msk, s, NEG_INF)
        ss.append(s)
    m = sink
    for s in ss:
        m = jnp.maximum(m, jnp.max(s, axis=-1, keepdims=True))
    den = jnp.exp(sink - m)
    o = None
    for s, v in zip(ss, vals):
        p = jnp.exp(s - m)
        den = den + jnp.sum(p, axis=-1, keepdims=True)
        pv = _dot(p.astype(BF16), v)
        o = pv if o is None else o + pv
    return o / den


def _win_kernel(sink_ref, q_ref, kp_ref, kc_ref, kn_ref, vp_ref, vc_ref, vn_ref,
                kx_ref, vx_ref, o_ref):
    n = pl.program_id(1)
    lo, hi = _lane_masks(BF16)
    qi = lax.broadcasted_iota(jnp.int32, (BLOCK, 3 * BLOCK), 0) + BLOCK
    kj = lax.broadcasted_iota(jnp.int32, (BLOCK, 3 * BLOCK), 1)
    kabs = n * BLOCK + kj - BLOCK
    mask = (jnp.abs(kj - qi) <= WINDOW) & (kabs >= 0) & (kabs < SEQ)
    for g in range(HKV_A):
        sl = slice(g * LANES, (g + 1) * LANES)
        kw = jnp.concatenate([kp_ref[:, sl], kc_ref[:, sl], kn_ref[:, sl]], axis=0)
        vw = jnp.concatenate([vp_ref[:, sl], vc_ref[:, sl], vn_ref[:, sl]], axis=0)
        kx = kx_ref[:, sl]
        vx = vx_ref[:, sl]
        for cc in range(2):
            c = 2 * g + cc
            qc = q_ref[:, c * LANES:(c + 1) * LANES]
            o_pair = None
            for e, hm in enumerate((lo, hi)):
                sink = jnp.full((1, 1), sink_ref[2 * c + e], F32)
                o = _gqa_head(qc * hm, [kw, kx], [vw * hm, vx * hm], sink, [mask, None])
                o_pair = o if o_pair is None else o_pair + o
            o_ref[:, c * LANES:(c + 1) * LANES] = o_pair.astype(o_ref.dtype)


def _window_attn(sink, qa, kdup, vdup, kdup_c, vdup_c, batch):
    nb = SEQ // BLOCK
    q_map = lambda b, n, s: (b * nb + n, 0)
    prev_map = lambda b, n, s: (b * nb + jnp.maximum(n - 1, 0), 0)
    next_map = lambda b, n, s: (b * nb + jnp.minimum(n + 1, nb - 1), 0)
    ctx_map = lambda b, n, s: (b, 0)
    kv_w = HKV_A * LANES
    grid_spec = pltpu.PrefetchScalarGridSpec(
        num_scalar_prefetch=1,
        grid=(batch, nb),
        in_specs=[
            pl.BlockSpec((BLOCK, H_A * HD_A), q_map),
            pl.BlockSpec((BLOCK, kv_w), prev_map),
            pl.BlockSpec((BLOCK, kv_w), q_map),
            pl.BlockSpec((BLOCK, kv_w), next_map),
            pl.BlockSpec((BLOCK, kv_w), prev_map),
            pl.BlockSpec((BLOCK, kv_w), q_map),
            pl.BlockSpec((BLOCK, kv_w), next_map),
            pl.BlockSpec((CTX_LEN, kv_w), ctx_map),
            pl.BlockSpec((CTX_LEN, kv_w), ctx_map),
        ],
        out_specs=pl.BlockSpec((BLOCK, H_A * HD_A), q_map),
    )
    return pl.pallas_call(
        _win_kernel,
        grid_spec=grid_spec,
        out_shape=jax.ShapeDtypeStruct((batch * SEQ, H_A * HD_A), BF16),
        compiler_params=pltpu.CompilerParams(
            dimension_semantics=("arbitrary", "arbitrary"), vmem_limit_bytes=VMEM_LIMIT),
        name="window_attn",
    )(sink, qa, kdup, kdup, kdup, vdup, vdup, vdup, kdup_c, vdup_c)


def _mla_kernel(q_ref, kx_ref, vx_ref, kc_ref, vc_ref, o_ref, m_scr, l_scr, acc_scr):
    tq = q_ref.shape[0]
    lo, hi = _lane_masks(BF16)
    lane = lax.broadcasted_iota(jnp.int32, (1, LANES), 1)
    lo_f = lane < LANES // 2

    m_scr[...] = jnp.full(m_scr.shape, NEG_INF, F32)
    l_scr[...] = jnp.zeros(l_scr.shape, F32)
    acc_scr[...] = jnp.zeros(acc_scr.shape, F32)

    def kv_tile(k_tile, v_tile):
        reps = k_tile.shape[0] // LANES
        for c in range(H_B // 2):
            vch = v_tile[:, c * LANES:(c + 1) * LANES]
            pv = None
            alphas = []
            for e, hm in enumerate((lo, hi)):
                hh = 2 * c + e
                sl = slice(hh * HEAD_PAD, (hh + 1) * HEAD_PAD)
                s = _dot_t(q_ref[:, sl], k_tile[:, sl])
                m_prev = m_scr[hh]
                m_next = jnp.maximum(m_prev, jnp.max(s, axis=-1, keepdims=True))
                alpha = jnp.exp2((m_prev - m_next) * EXP2_SCALE_B)
                p = jnp.exp2((s - pltpu.repeat(m_next, reps, 1)) * EXP2_SCALE_B)
                l_scr[hh] = alpha * l_scr[hh] + jnp.sum(p, axis=-1, keepdims=True)
                m_scr[hh] = m_next
                alphas.append(alpha)
                d = _dot(p.astype(BF16), vch * hm)
                pv = d if pv is None else pv + d
            alpha_pair = jnp.where(lo_f, alphas[0], alphas[1])
            acc_scr[c] = acc_scr[c] * alpha_pair + pv

    n_tiles = kx_ref.shape[0] // TK_MLA

    def body(j, carry):
        r0 = pl.multiple_of(j * TK_MLA, TK_MLA)
        kv_tile(kx_ref[pl.ds(r0, TK_MLA), :], vx_ref[pl.ds(r0, TK_MLA), :])
        return carry

    lax.fori_loop(0, n_tiles, body, 0)
    kv_tile(kc_ref[...], vc_ref[...])

    for c in range(H_B // 2):
        l_pair = jnp.where(lo_f, l_scr[2 * c], l_scr[2 * c + 1])
        o_ref[:, c * LANES:(c + 1) * LANES] = (acc_scr[c] / l_pair).astype(o_ref.dtype)


def _mla_attn(qb, kb, vb, kb_c, vb_c, batch):
    tq = TQ_MLA
    nq = SEQ // tq
    res = lambda shape, fn: pl.BlockSpec(shape, fn, pipeline_mode=pl.Buffered(1))
    return pl.pallas_call(
        _mla_kernel,
        grid=(batch, nq),
        in_specs=[
            pl.BlockSpec((tq, H_B * HEAD_PAD), lambda b, i: (b * nq + i, 0)),
            res((SEQ, H_B * HEAD_PAD), lambda b, i: (b, 0)),
            res((SEQ, H_B * V_B), lambda b, i: (b, 0)),
            res((CTX_LEN, H_B * HEAD_PAD), lambda b, i: (b, 0)),
            res((CTX_LEN, H_B * V_B), lambda b, i: (b, 0)),
        ],
        out_specs=pl.BlockSpec((tq, H_B * V_B), lambda b, i: (b * nq + i, 0)),
        out_shape=jax.ShapeDtypeStruct((batch * SEQ, H_B * V_B), BF16),
        scratch_shapes=[
            pltpu.VMEM((H_B, tq, LANES), F32),
            pltpu.VMEM((H_B, tq, LANES), F32),
            pltpu.VMEM((H_B // 2, tq, LANES), F32),
        ],
        compiler_params=pltpu.CompilerParams(
            dimension_semantics=("arbitrary", "arbitrary"), vmem_limit_bytes=VMEM_LIMIT),
        name="mla_attn",
    )(qb, kb, vb, kb_c, vb_c)


def _ctx_attn_kernel(sink_ref, qa_ref, kd_ref, vd_ref, qb_ref, kb_ref, vb_ref, ya_ref, yb_ref):
    lo, hi = _lane_masks(BF16)
    for c in range(H_A // 2):
        g = c // 2
        sl = slice(g * LANES, (g + 1) * LANES)
        kx = kd_ref[:, sl]
        vx = vd_ref[:, sl]
        qc = qa_ref[:, c * LANES:(c + 1) * LANES]
        o_pair = None
        for e, hm in enumerate((lo, hi)):
            sink = jnp.full((1, 1), sink_ref[2 * c + e], F32)
            o = _gqa_head(qc * hm, [kx], [vx * hm], sink, [None])
            o_pair = o if o_pair is None else o_pair + o
        ya_ref[:, c * LANES:(c + 1) * LANES] = o_pair.astype(ya_ref.dtype)
    for c in range(H_B // 2):
        vch = vb_ref[:, c * LANES:(c + 1) * LANES]
        o_pair = None
        for e, hm in enumerate((lo, hi)):
            hh = 2 * c + e
            sl = slice(hh * HEAD_PAD, (hh + 1) * HEAD_PAD)
            s = _dot_t(qb_ref[:, sl], kb_ref[:, sl])
            m = jnp.max(s, axis=-1, keepdims=True)
            p = jnp.exp2((s - m) * EXP2_SCALE_B)
            den = jnp.sum(p, axis=-1, keepdims=True)
            o = _dot(p.astype(BF16), vch * hm) / den
            o_pair = o if o_pair is None else o_pair + o
        yb_ref[:, c * LANES:(c + 1) * LANES] = o_pair.astype(yb_ref.dtype)


def _ctx_attn(sink, qa_c, kdup_c, vdup_c, qb_c, kb_c, vb_c, batch):
    blk = lambda cols: pl.BlockSpec((CTX_LEN, cols), lambda b, s: (b, 0))
    grid_spec = pltpu.PrefetchScalarGridSpec(
        num_scalar_prefetch=1,
        grid=(batch,),
        in_specs=[blk(512), blk(256), blk(256), blk(1024), blk(1024), blk(512)],
        out_specs=[blk(512), blk(512)],
    )
    return pl.pallas_call(
        _ctx_attn_kernel,
        grid_spec=grid_spec,
        out_shape=[jax.ShapeDtypeStruct((batch * CTX_LEN, 512), BF16)] * 2,
        compiler_params=pltpu.CompilerParams(
            dimension_semantics=("arbitrary",), vmem_limit_bytes=VMEM_LIMIT),
        name="ctx_attn",
    )(sink, qa_c, kdup_c, vdup_c, qb_c, kb_c, vb_c)


def _merge_kernel(ya_ref, yb_ref, sza_ref, szb_ref, szc_ref, bc_ref, cu_ref, cup_ref, cun_ref,
                  sga_ref, sgb_ref, sgc_ref, x_ref, gate_ref, gpost_ref, cw_ref, wbr_ref, wo_ref,
                  o_ref, *, tiles_per_seq):
    i = pl.program_id(0)
    tm = x_ref.shape[0]
    pos = i % tiles_per_seq
    z = cu_ref[...].astype(F32)
    prev_row = jnp.where(pos == 0, 0.0, cup_ref[7:8, :].astype(F32))
    next_row = jnp.where(pos == tiles_per_seq - 1, 0.0, cun_ref[0:1, :].astype(F32))
    row = lax.broadcasted_iota(jnp.int32, z.shape, 0)
    z_m1 = jnp.where(row == 0, prev_row, pltpu.roll(z, 1, 0))
    z_p1 = jnp.where(row == tm - 1, next_row, pltpu.roll(z, tm - 1, 0))
    conv = cw_ref[0:1, :] * z_m1 + cw_ref[1:2, :] * z + cw_ref[2:3, :] * z_p1
    yc = bc_ref[...].astype(F32) * conv

    def branch(y, sz_ref, sg_ref, k):
        t = (y * sz_ref[...].astype(F32)).astype(BF16)
        return sg_ref[...].astype(F32) * _dot(t, wbr_ref[k])

    m = (branch(ya_ref[...].astype(F32), sza_ref, sga_ref, 0)
         + branch(yb_ref[...].astype(F32), szb_ref, sgb_ref, 1)
         + branch(yc, szc_ref, sgc_ref, 2))
    out = _dot(m.astype(BF16), wo_ref[...])
    o_ref[...] = x_ref[...] + gate_ref[0] * _rms(out, gpost_ref[...])


def _merge(ya, yb, fr, resid, mod3, mod_row_fn, gpost, conv_w, wbr, wo, tiles_per_seq):
    (_, _, _, sza, _, _, _, szb, bc, cu, szc, sga, sgb, sgc) = fr
    t_rows = resid.shape[0]
    tm = TM_MERGE
    n_tiles = t_rows // tm
    sub = tm // 8
    n_sub = t_rows // 8
    row = lambda cols: pl.BlockSpec((tm, cols), lambda i: (i, 0))
    const = lambda shape: pl.BlockSpec(shape, lambda i: (0,) * len(shape),
                                       pipeline_mode=pl.Buffered(1))
    in_specs = [
        row(512), row(512), row(512), row(512), row(512), row(512), row(512),
        pl.BlockSpec((8, 512), lambda i: (jnp.maximum(i * sub - 1, 0), 0)),
        pl.BlockSpec((8, 512), lambda i: (jnp.minimum((i + 1) * sub, n_sub - 1), 0)),
        row(1024), row(1024), row(1024), row(1024),
        pl.BlockSpec((1, 1, D_MODEL), lambda i: (mod_row_fn(i), 0, 2)),
        const((1, D_MODEL)),
        const((3, W_C)),
        const((3, W_BRANCH, D_MODEL)),
        const((D_MODEL, D_MODEL)),
    ]
    return pl.pallas_call(
        functools.partial(_merge_kernel, tiles_per_seq=tiles_per_seq),
        grid=(n_tiles,),
        in_specs=in_specs,
        out_specs=pl.BlockSpec((tm, D_MODEL), lambda i: (i, 0)),
        out_shape=jax.ShapeDtypeStruct((t_rows, D_MODEL), F32),
        compiler_params=pltpu.CompilerParams(
            dimension_semantics=("arbitrary",), vmem_limit_bytes=VMEM_LIMIT),
        name="merge",
    )(ya, yb, sza, szb, szc, bc, cu, cu, cu, sga, sgb, sgc, resid, mod3, gpost, conv_w, wbr, wo)


def _pack_w_in(w):
    offs = np.concatenate([[0], np.cumsum(IN_SIZES)])
    seg = [w[:, offs[k]:offs[k + 1]] for k in range(len(IN_SIZES))]
    qa, ka, va, za, ql, kvl, kr, zb, bc, cc, uc, zc, ga, gb, gc = seg
    d = w.shape[0]
    dup = lambda t: jnp.concatenate([t[:, :HD_A], t[:, :HD_A], t[:, HD_A:], t[:, HD_A:]], axis=1)
    krp = jnp.concatenate([jnp.zeros((d, NOPE_B), w.dtype), kr,
                           jnp.zeros((d, HEAD_PAD - NOPE_B - ROPE_B), w.dtype)], axis=1)
    return jnp.concatenate([qa, dup(ka), dup(va), za, ql, kvl, krp, zb, bc, cc, uc, zc, ga, gb, gc],
                           axis=1).astype(BF16)


def _pack_w_qb(w):
    w3 = w.reshape(Q_LORA, H_B, NOPE_B + ROPE_B)
    w3 = jnp.pad(w3, ((0, 0), (0, 0), (0, HEAD_PAD - NOPE_B - ROPE_B)))
    return w3.reshape(Q_LORA, H_B * HEAD_PAD).astype(BF16)


def _pack_w_kvb(w):
    w3 = w.reshape(KV_LORA, H_B, NOPE_B + V_B)
    wkn = jnp.pad(w3[..., :NOPE_B], ((0, 0), (0, 0), (0, HEAD_PAD - NOPE_B)))
    wv = w3[..., NOPE_B:]
    return (wkn.reshape(KV_LORA, H_B * HEAD_PAD).astype(BF16),
            wv.reshape(KV_LORA, H_B * V_B).astype(BF16))


def _rope_tables(seq):
    t = jnp.arange(seq)
    row = (t // GRID_W).astype(F32)
    col = (t % GRID_W).astype(F32)

    def angles(rot_dim):
        axis_dim = rot_dim // 2
        inv = ROPE_BASE ** (-jnp.arange(0, axis_dim, 2, dtype=F32) / axis_dim)
        return jnp.concatenate([row[:, None] * inv, col[:, None] * inv], axis=-1)

    a = angles(HD_A)
    ca, sa = jnp.cos(a), jnp.sin(a)
    cos_a = jnp.tile(jnp.concatenate([ca, ca], axis=-1), (1, LANES // HD_A))
    sin_a = jnp.tile(jnp.concatenate([-sa, sa], axis=-1), (1, LANES // HD_A))
    b = angles(ROPE_B)
    cb, sb = jnp.cos(b), jnp.sin(b)
    tail = HEAD_PAD - NOPE_B - ROPE_B
    cos_b = jnp.concatenate([jnp.ones((seq, NOPE_B), F32), cb, cb, jnp.ones((seq, tail), F32)], axis=-1)
    sin_b = jnp.concatenate([jnp.zeros((seq, NOPE_B), F32), -sb, sb, jnp.zeros((seq, tail), F32)], axis=-1)
    return cos_a, sin_a, cos_b, sin_b


def kernel(x, c, ctx, c_ctx, w_mod, b_mod, g_pre, g_post, w_in, sink, g_qa, w_qb, g_kva, w_kvb,
           conv_w, w_branch, w_o):
    batch, seq, d = x.shape
    assert (seq, d) == (SEQ, D_MODEL) and ctx.shape == (batch, CTX_LEN, D_MODEL)
    assert batch <= 6
    depth = w_mod.shape[0]

    c_rows = jnp.concatenate([c, c_ctx[None, :], jnp.zeros((8 - batch - 1, d), F32)], axis=0)
    mod_all = _modulation(c_rows, w_mod, b_mod)
    tables = _rope_tables(seq)

    x2 = x.reshape(batch * seq, d)
    ctx2 = ctx.reshape(batch * CTX_LEN, d)
    x_tiles_front = seq // TM_FRONT
    x_tiles_merge = seq // TM_MERGE

    for i in range(depth):
        mod3 = mod_all[i].reshape(8, 1, 3 * d)
        w2 = _pack_w_in(w_in[i])
        wqb = _pack_w_qb(w_qb[i])
        wkn, wv = _pack_w_kvb(w_kvb[i])
        gpre = g_pre[i].reshape(1, d)
        gqa = g_qa[i].reshape(1, Q_LORA)
        gkva = g_kva[i].reshape(1, KV_LORA)
        gpost = g_post[i].reshape(1, d)
        wbr = w_branch[i].astype(BF16)
        wo = w_o[i].astype(BF16)

        fx = _front(x2, mod3, lambda t: t // x_tiles_front, gpre, w2, gqa, wqb, gkva, wkn, wv,
                    tables, x_tiles_front)
        fc = _front(ctx2, mod3, lambda t: batch, gpre, w2, gqa, wqb, gkva, wkn, wv, None, 1)
        qa, kdup, vdup, _, qb, kb, vb = fx[:7]
        qa_c, kdup_c, vdup_c, _, qb_c, kb_c, vb_c = fc[:7]

        ya = _window_attn(sink[i], qa, kdup, vdup, kdup_c, vdup_c, batch)
        yb = _mla_attn(qb, kb, vb, kb_c, vb_c, batch)
        x2_new = _merge(ya, yb, fx, x2, mod3, lambda t: t // x_tiles_merge, gpost, conv_w[i],
                        wbr, wo, x_tiles_merge)
        if i < depth - 1:
            ya_c, yb_c = _ctx_attn(sink[i], qa_c, kdup_c, vdup_c, qb_c, kb_c, vb_c, batch)
            ctx2 = _merge(ya_c, yb_c, fc, ctx2, mod3, lambda t: batch, gpost, conv_w[i],
                          wbr, wo, CTX_LEN // TM_MERGE)
        x2 = x2_new
    return x2.reshape(batch, seq, d)
```

```python
import functools

import jax
import jax.numpy as jnp
import numpy as np
from jax import lax
from jax.experimental import pallas as pl
from jax.experimental.pallas import tpu as pltpu

F32 = jnp.float32
BF16 = jnp.bfloat16

D_MODEL = 1024
SEQ = 8192
DEPTH = 2
CTX_LEN = 256
GRID_W = 64
WINDOW = 128
ROPE_BASE = 10000.0
EPS = 1e-6
NEG_INF = -1e30
LOG2E = float(np.log2(np.e))

H_A = 8
HKV_A = 2
G_A = H_A // HKV_A
HD_A = 64
QA_SCALE = HD_A ** -0.5 * LOG2E

H_B = 8
Q_LORA = 384
KV_LORA = 256
NOPE_B = 64
ROPE_B = 32
V_B = 64
QB_SCALE = (NOPE_B + ROPE_B) ** -0.5 * LOG2E

W_C = 512
W_BRANCH = 512

IN_SIZES = (H_A * HD_A, HKV_A * HD_A, HKV_A * HD_A, W_BRANCH,
            Q_LORA, KV_LORA, ROPE_B, W_BRANCH,
            W_C, W_C, W_C, W_BRANCH,
            D_MODEL, D_MODEL, D_MODEL)

LANES = 128
HEAD_PAD = 128

C_KA = 0
C_ZA = 128
C_QL = 640
C_KVL = 1024
C_KR = 1280
C_ZB = 1408
C_BC = 1920
C_CC = 2432
C_UC = 2944
C_ZC = 3456
C_GA = 3968
C_GB = 4992
C_GC = 6016
C_END = 7040
R_QA = 0
R_VA = H_A * HD_A
R_END = R_VA + HKV_A * HD_A

VMEM_LIMIT = 56 * 1024 * 1024

TM_FRONT = 256
TM_MERGE = 256
TQ_WIN = 256
TQ_MLA = 512
TK_MLA = 1024


def _silu(v):
    return v * jax.nn.sigmoid(v)


def _rms(v, g):
    return v * lax.rsqrt(jnp.mean(v * v, axis=-1, keepdims=True) + EPS) * g


def _dot(a, b):
    return jnp.dot(a, b, preferred_element_type=F32)


def _dot_t(a, b):
    return lax.dot_general(a, b, (((1,), (1,)), ((), ())), preferred_element_type=F32)


def _mod_kernel(c_ref, w_ref, b_ref, o_ref):
    sc = _silu(c_ref[...])
    o_ref[0] = jnp.dot(sc, w_ref[0], preferred_element_type=F32,
                       precision=lax.Precision.HIGHEST) + b_ref[0]


def _modulation(c_rows, w_mod, b_mod):
    n_chunk = 3
    return pl.pallas_call(
        _mod_kernel,
        grid=(DEPTH, n_chunk),
        in_specs=[
            pl.BlockSpec((8, D_MODEL), lambda l, j: (0, 0)),
            pl.BlockSpec((1, D_MODEL, D_MODEL), lambda l, j: (l, 0, j)),
            pl.BlockSpec((1, 1, D_MODEL), lambda l, j: (l, 0, j)),
        ],
        out_specs=pl.BlockSpec((1, 8, D_MODEL), lambda l, j: (l, 0, j)),
        out_shape=jax.ShapeDtypeStruct((DEPTH, 8, 3 * D_MODEL), F32),
        compiler_params=pltpu.CompilerParams(
            dimension_semantics=("arbitrary", "arbitrary"), vmem_limit_bytes=VMEM_LIMIT),
        name="modulation",
    )(c_rows, w_mod, b_mod.reshape(DEPTH, 1, 3 * D_MODEL))


def _rope_lanes(v, cos, sin, first_half, up_shift, dn_shift):
    rot = jnp.where(first_half, pltpu.roll(v, up_shift, 1), pltpu.roll(v, dn_shift, 1))
    return v * cos + rot * sin


def _rope_rows(x1, x2, cos, sin):
    return x1 * cos - x2 * sin, x2 * cos + x1 * sin


def _front_kernel(x_ref, shift_ref, scale_ref, gpre_ref, w_ref, wt_ref, gqa_ref, wqbt_ref,
                  gkva_ref, wkn_ref, wvt_ref, *rest, rope):
    if rope:
        (cosa_ref, sina_ref, cosb_ref, sinb_ref,
         cosat_ref, sinat_ref, cosbt_ref, sinbt_ref) = rest[:8]
        outs = rest[8:]
    else:
        outs = rest
    (qat_ref, ka_ref, vat_ref, sza_ref, qbt_ref, kb_ref, vbt_ref, szb_ref,
     bc_ref, cu_ref, szc_ref, sga_ref, sgb_ref, sgc_ref) = outs

    x = x_ref[...]
    h = _rms(x, gpre_ref[...]) * (1.0 + scale_ref[0]) + shift_ref[0]
    h = h.astype(BF16)

    def mm(a, b):
        return _dot(h, w_ref[:, a:b])

    qv_t = _dot_t(wt_ref[...], h)
    half = HD_A // 2
    for hh in range(H_A):
        r0 = R_QA + hh * HD_A
        x1, x2 = qv_t[r0:r0 + half], qv_t[r0 + half:r0 + HD_A]
        if rope:
            x1, x2 = _rope_rows(x1, x2, cosat_ref[...], sinat_ref[...])
        qat_ref[hh * HD_A:hh * HD_A + half, :] = (x1 * QA_SCALE).astype(BF16)
        qat_ref[hh * HD_A + half:(hh + 1) * HD_A, :] = (x2 * QA_SCALE).astype(BF16)
    vat_ref[...] = qv_t[R_VA:R_END].astype(BF16)
    ka = mm(C_KA, C_ZA)
    if rope:
        lane = lax.broadcasted_iota(jnp.int32, (1, LANES), 1)
        ka = _rope_lanes(ka, cosa_ref[...], sina_ref[...], (lane % HD_A) < half,
                         LANES - half, half)
    ka_ref[...] = ka.astype(BF16)
    sza_ref[...] = _silu(mm(C_ZA, C_QL)).astype(BF16)

    qn = _rms(mm(C_QL, C_KVL), gqa_ref[...]).astype(BF16)
    qb_t = _dot_t(wqbt_ref[...], qn)
    hr = ROPE_B // 2
    for hh in range(H_B):
        r0 = hh * HEAD_PAD
        x1 = qb_t[r0 + NOPE_B:r0 + NOPE_B + hr]
        x2 = qb_t[r0 + NOPE_B + hr:r0 + NOPE_B + ROPE_B]
        if rope:
            x1, x2 = _rope_rows(x1, x2, cosbt_ref[...], sinbt_ref[...])
        qbt_ref[r0:r0 + NOPE_B, :] = (qb_t[r0:r0 + NOPE_B] * QB_SCALE).astype(BF16)
        qbt_ref[r0 + NOPE_B:r0 + NOPE_B + hr, :] = (x1 * QB_SCALE).astype(BF16)
        qbt_ref[r0 + NOPE_B + hr:r0 + NOPE_B + ROPE_B, :] = (x2 * QB_SCALE).astype(BF16)
        qbt_ref[r0 + NOPE_B + ROPE_B:r0 + HEAD_PAD, :] = jnp.zeros(
            (HEAD_PAD - NOPE_B - ROPE_B, x.shape[0]), BF16)
    kvn = _rms(mm(C_KVL, C_KR), gkva_ref[...]).astype(BF16)
    kn = _dot(kvn, wkn_ref[...])
    kr = mm(C_KR, C_ZB)
    if rope:
        kr = _rope_lanes(kr, cosb_ref[...], sinb_ref[...], lane < (NOPE_B + hr),
                         LANES - hr, hr)
    for hh in range(H_B):
        sl = slice(hh * HEAD_PAD, (hh + 1) * HEAD_PAD)
        kb_ref[:, sl] = (kn[:, sl] + kr).astype(BF16)
    vbt_ref[...] = _dot_t(wvt_ref[...], kvn).astype(BF16)
    szb_ref[...] = _silu(mm(C_ZB, C_BC)).astype(BF16)

    bc_ref[...] = mm(C_BC, C_CC).astype(BF16)
    cu_ref[...] = (mm(C_CC, C_UC) * mm(C_UC, C_ZC)).astype(BF16)
    szc_ref[...] = _silu(mm(C_ZC, C_GA)).astype(BF16)

    sga_ref[...] = jax.nn.sigmoid(mm(C_GA, C_GB)).astype(BF16)
    sgb_ref[...] = jax.nn.sigmoid(mm(C_GB, C_GC)).astype(BF16)
    sgc_ref[...] = jax.nn.sigmoid(mm(C_GC, C_END)).astype(BF16)


_FRONT_OUTS = ((True, H_A * HD_A), (False, HKV_A * HD_A), (True, HKV_A * HD_A), (False, W_BRANCH),
               (True, H_B * HEAD_PAD), (False, H_B * HEAD_PAD), (True, H_B * V_B), (False, W_BRANCH),
               (False, W_C), (False, W_C), (False, W_BRANCH),
               (False, D_MODEL), (False, D_MODEL), (False, D_MODEL))


def _front(rows, mod3, mod_row_fn, gpre, w2, wt, gqa, wqbt, gkva, wkn, wvt, tables, tiles_per_seq):
    t_rows = rows.shape[0]
    tm = TM_FRONT
    n_tiles = t_rows // tm
    rope = tables is not None
    const = lambda shape: pl.BlockSpec(shape, lambda i: (0,) * len(shape),
                                       pipeline_mode=pl.Buffered(1))
    in_specs = [
        pl.BlockSpec((tm, D_MODEL), lambda i: (i, 0)),
        pl.BlockSpec((1, 1, D_MODEL), lambda i: (mod_row_fn(i), 0, 0)),
        pl.BlockSpec((1, 1, D_MODEL), lambda i: (mod_row_fn(i), 0, 1)),
        const((1, D_MODEL)),
        const((D_MODEL, C_END)),
        const((R_END, D_MODEL)),
        const((1, Q_LORA)),
        const((H_B * HEAD_PAD, Q_LORA)),
        const((1, KV_LORA)),
        const((KV_LORA, H_B * HEAD_PAD)),
        const((H_B * V_B, KV_LORA)),
    ]
    args = [rows, mod3, mod3, gpre, w2, wt, gqa, wqbt, gkva, wkn, wvt]
    if rope:
        pos = lambda i: i % tiles_per_seq
        in_specs += [pl.BlockSpec((tm, LANES), lambda i: (pos(i), 0))] * 4
        in_specs += [pl.BlockSpec((HD_A // 2, tm), lambda i: (0, pos(i)))] * 2
        in_specs += [pl.BlockSpec((ROPE_B // 2, tm), lambda i: (0, pos(i)))] * 2
        args += list(tables)
    out_specs, out_shape = [], []
    for transposed, width in _FRONT_OUTS:
        if transposed:
            out_specs.append(pl.BlockSpec((width, tm), lambda i: (0, i)))
            out_shape.append(jax.ShapeDtypeStruct((width, t_rows), BF16))
        else:
            out_specs.append(pl.BlockSpec((tm, width), lambda i: (i, 0)))
            out_shape.append(jax.ShapeDtypeStruct((t_rows, width), BF16))
    return pl.pallas_call(
        functools.partial(_front_kernel, rope=rope),
        grid=(n_tiles,),
        in_specs=in_specs,
        out_specs=out_specs,
        out_shape=out_shape,
        compiler_params=pltpu.CompilerParams(
            dimension_semantics=("arbitrary",), vmem_limit_bytes=VMEM_LIMIT),
        name="front_rope" if rope else "front_ctx",
    )(*args)


def _gqa_heads(sink_ref, qt_ref, k_all, vt_all, mask, oT_scr):
    tq = qt_ref.shape[1]
    zeros = jnp.zeros((HD_A, tq), BF16)
    for hh in range(H_A):
        g = hh // G_A
        q_t = qt_ref[hh * HD_A:(hh + 1) * HD_A, :]
        w = jnp.concatenate([q_t, zeros] if g == 0 else [zeros, q_t], axis=0)
        s_t = _dot(k_all, w)
        if mask is not None:
            s_t = jnp.where(mask, s_t, NEG_INF)
        sink = sink_ref[hh] * LOG2E
        m = jnp.maximum(jnp.max(s_t, axis=0, keepdims=True), sink)
        p_t = jnp.exp2(s_t - m)
        den = jnp.sum(p_t, axis=0, keepdims=True) + jnp.exp2(sink - m)
        o_t = _dot(vt_all[g * HD_A:(g + 1) * HD_A, :], p_t.astype(BF16))
        oT_scr[hh * HD_A:(hh + 1) * HD_A, :] = o_t / den


def _mla_tile(qt_ref, k_fn, vt_fn, m_scr, l_scr, acc_scr):
    def scores(hh):
        return _dot(k_fn(hh), qt_ref[hh * HEAD_PAD:(hh + 1) * HEAD_PAD, :])

    s_next = scores(0)
    for hh in range(H_B):
        s_t = s_next
        if hh + 1 < H_B:
            s_next = scores(hh + 1)
        m_prev = m_scr[hh]
        m_new = jnp.maximum(m_prev, jnp.max(s_t, axis=0, keepdims=True))
        alpha = jnp.exp2(m_prev - m_new)
        p_t = jnp.exp2(s_t - m_new)
        l_scr[hh] = alpha * l_scr[hh] + jnp.sum(p_t, axis=0, keepdims=True)
        m_scr[hh] = m_new
        rows = slice(hh * V_B, (hh + 1) * V_B)
        acc_scr[rows, :] = alpha * acc_scr[rows, :] + _dot(vt_fn(hh), p_t.astype(BF16))


def _mla_init(m_scr, l_scr, acc_scr):
    m_scr[...] = jnp.full(m_scr.shape, NEG_INF, F32)
    l_scr[...] = jnp.zeros(l_scr.shape, F32)
    acc_scr[...] = jnp.zeros(acc_scr.shape, F32)


def _mla_finish(o_ref, l_scr, acc_scr):
    for hh in range(H_B):
        rows = slice(hh * V_B, (hh + 1) * V_B)
        acc_scr[rows, :] = acc_scr[rows, :] / l_scr[hh]
    o_ref[...] = acc_scr[...].T.astype(o_ref.dtype)


def _win_kernel(sink_ref, qt_ref, kp_ref, kc_ref, kn_ref, kx_ref, vp_ref, vc_ref, vn_ref, vx_ref,
                o_ref, oT_scr):
    n = pl.program_id(1)
    tq = qt_ref.shape[1]
    n_loc = tq + 2 * WINDOW
    k_all = jnp.concatenate([kp_ref[...], kc_ref[...], kn_ref[...], kx_ref[...]], axis=0)
    vt_all = jnp.concatenate([vp_ref[...], vc_ref[...], vn_ref[...], vx_ref[...]], axis=1)
    nk = k_all.shape[0]
    r = lax.broadcasted_iota(jnp.int32, (nk, tq), 0)
    c = lax.broadcasted_iota(jnp.int32, (nk, tq), 1)
    kabs = n * tq - WINDOW + r
    in_band = (jnp.abs(r - WINDOW - c) <= WINDOW) & (kabs >= 0) & (kabs < SEQ)
    mask = in_band | (r >= n_loc)
    _gqa_heads(sink_ref, qt_ref, k_all, vt_all, mask, oT_scr)
    o_ref[...] = oT_scr[...].T.astype(o_ref.dtype)


def _window_attn(sink, qat, ka, vat, ka_c, vat_c, batch):
    tq = TQ_WIN
    nq = SEQ // tq
    sub = tq // WINDOW
    nw = SEQ // WINDOW
    kv_w = HKV_A * HD_A
    prev_i = lambda b, n: b * nw + jnp.maximum(n * sub - 1, 0)
    next_i = lambda b, n: b * nw + jnp.minimum((n + 1) * sub, nw - 1)
    grid_spec = pltpu.PrefetchScalarGridSpec(
        num_scalar_prefetch=1,
        grid=(batch, nq),
        in_specs=[
            pl.BlockSpec((H_A * HD_A, tq), lambda b, n, s: (0, b * nq + n)),
            pl.BlockSpec((WINDOW, kv_w), lambda b, n, s: (prev_i(b, n), 0)),
            pl.BlockSpec((tq, kv_w), lambda b, n, s: (b * nq + n, 0)),
            pl.BlockSpec((WINDOW, kv_w), lambda b, n, s: (next_i(b, n), 0)),
            pl.BlockSpec((CTX_LEN, kv_w), lambda b, n, s: (b, 0)),
            pl.BlockSpec((kv_w, WINDOW), lambda b, n, s: (0, prev_i(b, n))),
            pl.BlockSpec((kv_w, tq), lambda b, n, s: (0, b * nq + n)),
            pl.BlockSpec((kv_w, WINDOW), lambda b, n, s: (0, next_i(b, n))),
            pl.BlockSpec((kv_w, CTX_LEN), lambda b, n, s: (0, b)),
        ],
        out_specs=pl.BlockSpec((tq, H_A * HD_A), lambda b, n, s: (b * nq + n, 0)),
        scratch_shapes=[pltpu.VMEM((H_A * HD_A, tq), F32)],
    )
    return pl.pallas_call(
        _win_kernel,
        grid_spec=grid_spec,
        out_shape=jax.ShapeDtypeStruct((batch * SEQ, H_A * HD_A), BF16),
        compiler_params=pltpu.CompilerParams(
            dimension_semantics=("arbitrary", "arbitrary"), vmem_limit_bytes=VMEM_LIMIT),
        name="window_attn",
    )(sink, qat, ka, ka, ka, ka_c, vat, vat, vat, vat_c)


def _mla_kernel(qt_ref, kx_ref, vxt_ref, kc_ref, vct_ref, o_ref, m_scr, l_scr, acc_scr):
    _mla_init(m_scr, l_scr, acc_scr)
    n_tiles = kx_ref.shape[0] // TK_MLA

    def body(j, carry):
        r0 = pl.multiple_of(j * TK_MLA, TK_MLA)
        _mla_tile(qt_ref,
                  lambda hh: kx_ref[pl.ds(r0, TK_MLA), hh * HEAD_PAD:(hh + 1) * HEAD_PAD],
                  lambda hh: vxt_ref[hh * V_B:(hh + 1) * V_B, pl.ds(r0, TK_MLA)],
                  m_scr, l_scr, acc_scr)
        return carry

    lax.fori_loop(0, n_tiles, body, 0)
    _mla_tile(qt_ref,
              lambda hh: kc_ref[:, hh * HEAD_PAD:(hh + 1) * HEAD_PAD],
              lambda hh: vct_ref[hh * V_B:(hh + 1) * V_B, :],
              m_scr, l_scr, acc_scr)
    _mla_finish(o_ref, l_scr, acc_scr)


def _mla_scratch(tq):
    return [pltpu.VMEM((H_B, 1, tq), F32), pltpu.VMEM((H_B, 1, tq), F32),
            pltpu.VMEM((H_B * V_B, tq), F32)]


def _mla_attn(qbt, kb, vbt, kb_c, vbt_c, batch):
    tq = TQ_MLA
    nq = SEQ // tq
    res = lambda shape, fn: pl.BlockSpec(shape, fn, pipeline_mode=pl.Buffered(1))
    return pl.pallas_call(
        _mla_kernel,
        grid=(batch, nq),
        in_specs=[
            pl.BlockSpec((H_B * HEAD_PAD, tq), lambda b, i: (0, b * nq + i)),
            res((SEQ, H_B * HEAD_PAD), lambda b, i: (b, 0)),
            res((H_B * V_B, SEQ), lambda b, i: (0, b)),
            res((CTX_LEN, H_B * HEAD_PAD), lambda b, i: (b, 0)),
            res((H_B * V_B, CTX_LEN), lambda b, i: (0, b)),
        ],
        out_specs=pl.BlockSpec((tq, H_B * V_B), lambda b, i: (b * nq + i, 0)),
        out_shape=jax.ShapeDtypeStruct((batch * SEQ, H_B * V_B), BF16),
        scratch_shapes=_mla_scratch(tq),
        compiler_params=pltpu.CompilerParams(
            dimension_semantics=("arbitrary", "arbitrary"), vmem_limit_bytes=VMEM_LIMIT),
        name="mla_attn",
    )(qbt, kb, vbt, kb_c, vbt_c)


def _ctx_attn_kernel(sink_ref, qat_ref, ka_ref, vat_ref, qbt_ref, kb_ref, vbt_ref,
                     ya_ref, yb_ref, oT_scr, m_scr, l_scr, acc_scr):
    _gqa_heads(sink_ref, qat_ref, ka_ref[...], vat_ref[...], None, oT_scr)
    ya_ref[...] = oT_scr[...].T.astype(ya_ref.dtype)
    _mla_init(m_scr, l_scr, acc_scr)
    _mla_tile(qbt_ref,
              lambda hh: kb_ref[:, hh * HEAD_PAD:(hh + 1) * HEAD_PAD],
              lambda hh: vbt_ref[hh * V_B:(hh + 1) * V_B, :],
              m_scr, l_scr, acc_scr)
    _mla_finish(yb_ref, l_scr, acc_scr)


def _ctx_attn(sink, qat_c, ka_c, vat_c, qbt_c, kb_c, vbt_c, batch):
    tok = lambda cols: pl.BlockSpec((CTX_LEN, cols), lambda b, s: (b, 0))
    feat = lambda rows: pl.BlockSpec((rows, CTX_LEN), lambda b, s: (0, b))
    grid_spec = pltpu.PrefetchScalarGridSpec(
        num_scalar_prefetch=1,
        grid=(batch,),
        in_specs=[feat(H_A * HD_A), tok(HKV_A * HD_A), feat(HKV_A * HD_A),
                  feat(H_B * HEAD_PAD), tok(H_B * HEAD_PAD), feat(H_B * V_B)],
        out_specs=[tok(H_A * HD_A), tok(H_B * V_B)],
        scratch_shapes=[pltpu.VMEM((H_A * HD_A, CTX_LEN), F32)] + _mla_scratch(CTX_LEN),
    )
    return pl.pallas_call(
        _ctx_attn_kernel,
        grid_spec=grid_spec,
        out_shape=[jax.ShapeDtypeStruct((batch * CTX_LEN, W_BRANCH), BF16)] * 2,
        compiler_params=pltpu.CompilerParams(
            dimension_semantics=("arbitrary",), vmem_limit_bytes=VMEM_LIMIT),
        name="ctx_attn",
    )(sink, qat_c, ka_c, vat_c, qbt_c, kb_c, vbt_c)


def _merge_kernel(ya_ref, yb_ref, sza_ref, szb_ref, szc_ref, bc_ref, cu_ref, cup_ref, cun_ref,
                  sga_ref, sgb_ref, sgc_ref, x_ref, gate_ref, gpost_ref, cw_ref, wbr_ref, wo_ref,
                  o_ref, *, tiles_per_seq):
    i = pl.program_id(0)
    tm = x_ref.shape[0]
    pos = i % tiles_per_seq
    z = cu_ref[...].astype(F32)
    prev_row = jnp.where(pos == 0, 0.0, cup_ref[7:8, :].astype(F32))
    next_row = jnp.where(pos == tiles_per_seq - 1, 0.0, cun_ref[0:1, :].astype(F32))
    row = lax.broadcasted_iota(jnp.int32, z.shape, 0)
    z_m1 = jnp.where(row == 0, prev_row, pltpu.roll(z, 1, 0))
    z_p1 = jnp.where(row == tm - 1, next_row, pltpu.roll(z, tm - 1, 0))
    conv = cw_ref[0:1, :] * z_m1 + cw_ref[1:2, :] * z + cw_ref[2:3, :] * z_p1
    yc = bc_ref[...].astype(F32) * conv

    def branch(y, sz_ref, sg_ref, k):
        t = (y * sz_ref[...].astype(F32)).astype(BF16)
        return sg_ref[...].astype(F32) * _dot(t, wbr_ref[k])

    m = (branch(ya_ref[...].astype(F32), sza_ref, sga_ref, 0)
         + branch(yb_ref[...].astype(F32), szb_ref, sgb_ref, 1)
         + branch(yc, szc_ref, sgc_ref, 2))
    out = _dot(m.astype(BF16), wo_ref[...])
    o_ref[...] = x_ref[...] + gate_ref[0] * _rms(out, gpost_ref[...])


def _merge(ya, yb, fr, resid, mod3, mod_row_fn, gpost, conv_w, wbr, wo, tiles_per_seq):
    (_, _, _, sza, _, _, _, szb, bc, cu, szc, sga, sgb, sgc) = fr
    t_rows = resid.shape[0]
    tm = TM_MERGE
    n_tiles = t_rows // tm
    sub = tm // 8
    n_sub = t_rows // 8
    row = lambda cols: pl.BlockSpec((tm, cols), lambda i: (i, 0))
    const = lambda shape: pl.BlockSpec(shape, lambda i: (0,) * len(shape),
                                       pipeline_mode=pl.Buffered(1))
    in_specs = [
        row(512), row(512), row(512), row(512), row(512), row(512), row(512),
        pl.BlockSpec((8, 512), lambda i: (jnp.maximum(i * sub - 1, 0), 0)),
        pl.BlockSpec((8, 512), lambda i: (jnp.minimum((i + 1) * sub, n_sub - 1), 0)),
        row(1024), row(1024), row(1024), row(1024),
        pl.BlockSpec((1, 1, D_MODEL), lambda i: (mod_row_fn(i), 0, 2)),
        const((1, D_MODEL)),
        const((3, W_C)),
        const((3, W_BRANCH, D_MODEL)),
        const((D_MODEL, D_MODEL)),
    ]
    return pl.pallas_call(
        functools.partial(_merge_kernel, tiles_per_seq=tiles_per_seq),
        grid=(n_tiles,),
        in_specs=in_specs,
        out_specs=pl.BlockSpec((tm, D_MODEL), lambda i: (i, 0)),
        out_shape=jax.ShapeDtypeStruct((t_rows, D_MODEL), F32),
        compiler_params=pltpu.CompilerParams(
            dimension_semantics=("arbitrary",), vmem_limit_bytes=VMEM_LIMIT),
        name="merge",
    )(ya, yb, sza, szb, szc, bc, cu, cu, cu, sga, sgb, sgc, resid, mod3, gpost, conv_w, wbr, wo)


def _pack_w_in(w):
    offs = np.concatenate([[0], np.cumsum(IN_SIZES)])
    seg = [w[:, offs[k]:offs[k + 1]] for k in range(len(IN_SIZES))]
    qa, ka, va, za, ql, kvl, kr, zb, bc, cc, uc, zc, ga, gb, gc = seg
    d = w.shape[0]
    krp = jnp.concatenate([jnp.zeros((d, NOPE_B), w.dtype), kr,
                           jnp.zeros((d, HEAD_PAD - NOPE_B - ROPE_B), w.dtype)], axis=1)
    w2 = jnp.concatenate([ka, za, ql, kvl, krp, zb, bc, cc, uc, zc, ga, gb, gc], axis=1)
    wt = jnp.concatenate([qa, va], axis=1).T
    return w2.astype(BF16), wt.astype(BF16)


def _pack_w_qb(w):
    w3 = w.reshape(Q_LORA, H_B, NOPE_B + ROPE_B)
    w3 = jnp.pad(w3, ((0, 0), (0, 0), (0, HEAD_PAD - NOPE_B - ROPE_B)))
    return w3.reshape(Q_LORA, H_B * HEAD_PAD).T.astype(BF16)


def _pack_w_kvb(w):
    w3 = w.reshape(KV_LORA, H_B, NOPE_B + V_B)
    wkn = jnp.pad(w3[..., :NOPE_B], ((0, 0), (0, 0), (0, HEAD_PAD - NOPE_B)))
    wv = w3[..., NOPE_B:]
    return (wkn.reshape(KV_LORA, H_B * HEAD_PAD).astype(BF16),
            wv.reshape(KV_LORA, H_B * V_B).T.astype(BF16))


def _rope_tables(seq):
    t = jnp.arange(seq)
    row = (t // GRID_W).astype(F32)
    col = (t % GRID_W).astype(F32)

    def angles(rot_dim):
        axis_dim = rot_dim // 2
        inv = ROPE_BASE ** (-jnp.arange(0, axis_dim, 2, dtype=F32) / axis_dim)
        return jnp.concatenate([row[:, None] * inv, col[:, None] * inv], axis=-1)

    a = angles(HD_A)
    ca, sa = jnp.cos(a), jnp.sin(a)
    cos_a = jnp.tile(jnp.concatenate([ca, ca], axis=-1), (1, LANES // HD_A))
    sin_a = jnp.tile(jnp.concatenate([-sa, sa], axis=-1), (1, LANES // HD_A))
    b = angles(ROPE_B)
    cb, sb = jnp.cos(b), jnp.sin(b)
    tail = HEAD_PAD - NOPE_B - ROPE_B
    cos_b = jnp.concatenate([jnp.ones((seq, NOPE_B), F32), cb, cb, jnp.ones((seq, tail), F32)], axis=-1)
    sin_b = jnp.concatenate([jnp.zeros((seq, NOPE_B), F32), -sb, sb, jnp.zeros((seq, tail), F32)], axis=-1)
    return cos_a, sin_a, cos_b, sin_b, ca.T, sa.T, cb.T, sb.T


def kernel(x, c, ctx, c_ctx, w_mod, b_mod, g_pre, g_post, w_in, sink, g_qa, w_qb, g_kva, w_kvb,
           conv_w, w_branch, w_o):
    batch, seq, d = x.shape
    assert (seq, d) == (SEQ, D_MODEL) and ctx.shape == (batch, CTX_LEN, D_MODEL)
    assert batch <= 6
    depth = w_mod.shape[0]

    c_rows = jnp.concatenate([c, c_ctx[None, :], jnp.zeros((8 - batch - 1, d), F32)], axis=0)
    mod_all = _modulation(c_rows, w_mod, b_mod)
    tables = _rope_tables(seq)

    x2 = x.reshape(batch * seq, d)
    ctx2 = ctx.reshape(batch * CTX_LEN, d)
    x_tiles_front = seq // TM_FRONT
    x_tiles_merge = seq // TM_MERGE

    for i in range(depth):
        mod3 = mod_all[i].reshape(8, 1, 3 * d)
        w2, wt = _pack_w_in(w_in[i])
        wqbt = _pack_w_qb(w_qb[i])
        wkn, wvt = _pack_w_kvb(w_kvb[i])
        gpre = g_pre[i].reshape(1, d)
        gqa = g_qa[i].reshape(1, Q_LORA)
        gkva = g_kva[i].reshape(1, KV_LORA)
        gpost = g_post[i].reshape(1, d)
        wbr = w_branch[i].astype(BF16)
        wo = w_o[i].astype(BF16)

        fx = _front(x2, mod3, lambda t: t // x_tiles_front, gpre, w2, wt, gqa, wqbt, gkva, wkn,
                    wvt, tables, x_tiles_front)
        fc = _front(ctx2, mod3, lambda t: batch, gpre, w2, wt, gqa, wqbt, gkva, wkn, wvt, None, 1)
        qat, ka, vat, _, qbt, kb, vbt = fx[:7]
        qat_c, ka_c, vat_c, _, qbt_c, kb_c, vbt_c = fc[:7]

        ya = _window_attn(sink[i], qat, ka, vat, ka_c, vat_c, batch)
        yb = _mla_attn(qbt, kb, vbt, kb_c, vbt_c, batch)
        x2_new = _merge(ya, yb, fx, x2, mod3, lambda t: t // x_tiles_merge, gpost, conv_w[i],
                        wbr, wo, x_tiles_merge)
        if i < depth - 1:
            ya_c, yb_c = _ctx_attn(sink[i], qat_c, ka_c, vat_c, qbt_c, kb_c, vbt_c, batch)
            ctx2 = _merge(ya_c, yb_c, fc, ctx2, mod3, lambda t: batch, gpost, conv_w[i],
                          wbr, wo, CTX_LEN // TM_MERGE)
        x2 = x2_new
    return x2.reshape(batch, seq, d)
```

```python
import functools

import jax
import jax.numpy as jnp
import numpy as np
from jax import lax
from jax.experimental import pallas as pl
from jax.experimental.pallas import tpu as pltpu

F32 = jnp.float32
BF16 = jnp.bfloat16

D_MODEL = 1024
SEQ = 8192
DEPTH = 2
CTX_LEN = 256
GRID_W = 64
WINDOW = 128
ROPE_BASE = 10000.0
EPS = 1e-6
NEG_INF = -1e30
LOG2E = float(np.log2(np.e))

H_A = 8
HKV_A = 2
G_A = H_A // HKV_A
HD_A = 64
QA_SCALE = HD_A ** -0.5 * LOG2E

H_B = 8
Q_LORA = 384
KV_LORA = 256
NOPE_B = 64
ROPE_B = 32
V_B = 64
QB_SCALE = (NOPE_B + ROPE_B) ** -0.5 * LOG2E

W_C = 512
W_BRANCH = 512

IN_SIZES = (H_A * HD_A, HKV_A * HD_A, HKV_A * HD_A, W_BRANCH,
            Q_LORA, KV_LORA, ROPE_B, W_BRANCH,
            W_C, W_C, W_C, W_BRANCH,
            D_MODEL, D_MODEL, D_MODEL)

LANES = 128
HEAD_PAD = 128

C_KA = 0
C_ZA = 128
C_QL = 640
C_KVL = 1024
C_KR = 1280
C_ZB = 1408
C_BC = 1920
C_CC = 2432
C_UC = 2944
C_ZC = 3456
C_GA = 3968
C_GB = 4992
C_GC = 6016
C_END = 7040
R_QA = 0
R_VA = H_A * HD_A
R_END = R_VA + HKV_A * HD_A

VMEM_LIMIT = 56 * 1024 * 1024

TM_FRONT = 512
TM_MERGE = 256
TQ_WIN = 256
TQ_MLA = 512
TK_MLA = 2048
KC_MLA = 512


def _silu(v):
    return v * jax.nn.sigmoid(v)


def _rms(v, g):
    return v * lax.rsqrt(jnp.mean(v * v, axis=-1, keepdims=True) + EPS) * g


def _dot(a, b):
    return jnp.dot(a, b, preferred_element_type=F32)


def _dot_t(a, b):
    return lax.dot_general(a, b, (((1,), (1,)), ((), ())), preferred_element_type=F32)


def _mod_kernel(c_ref, w_ref, b_ref, o_ref):
    sc = _silu(c_ref[...])
    o_ref[0] = jnp.dot(sc, w_ref[0], preferred_element_type=F32,
                       precision=lax.Precision.HIGHEST) + b_ref[0]


def _modulation(c_rows, w_mod, b_mod):
    n_chunk = 3
    return pl.pallas_call(
        _mod_kernel,
        grid=(DEPTH, n_chunk),
        in_specs=[
            pl.BlockSpec((8, D_MODEL), lambda l, j: (0, 0)),
            pl.BlockSpec((1, D_MODEL, D_MODEL), lambda l, j: (l, 0, j)),
            pl.BlockSpec((1, 1, D_MODEL), lambda l, j: (l, 0, j)),
        ],
        out_specs=pl.BlockSpec((1, 8, D_MODEL), lambda l, j: (l, 0, j)),
        out_shape=jax.ShapeDtypeStruct((DEPTH, 8, 3 * D_MODEL), F32),
        compiler_params=pltpu.CompilerParams(
            dimension_semantics=("arbitrary", "arbitrary"), vmem_limit_bytes=VMEM_LIMIT),
        name="modulation",
    )(c_rows, w_mod, b_mod.reshape(DEPTH, 1, 3 * D_MODEL))


def _rope_lanes(v, cos, sin, first_half, up_shift, dn_shift):
    rot = jnp.where(first_half, pltpu.roll(v, up_shift, 1), pltpu.roll(v, dn_shift, 1))
    return v * cos + rot * sin


def _rope_rows(x1, x2, cos, sin):
    return x1 * cos - x2 * sin, x2 * cos + x1 * sin


def _front_kernel(x_ref, shift_ref, scale_ref, gpre_ref, w_ref, wt_ref, gqa_ref, wqbt_ref,
                  gkva_ref, wkn_ref, wvt_ref, *rest, rope):
    if rope:
        (cosa_ref, sina_ref, cosb_ref, sinb_ref,
         cosat_ref, sinat_ref, cosbt_ref, sinbt_ref) = rest[:8]
        outs = rest[8:]
    else:
        outs = rest
    (qat_ref, ka_ref, vat_ref, sza_ref, qbt_ref, kb_ref, vbt_ref, szb_ref,
     bc_ref, cu_ref, szc_ref, sga_ref, sgb_ref, sgc_ref) = outs

    x = x_ref[...]
    h = _rms(x, gpre_ref[...]) * (1.0 + scale_ref[0]) + shift_ref[0]
    h = h.astype(BF16)

    def mm(a, b):
        return _dot(h, w_ref[:, a:b])

    qv_t = _dot_t(wt_ref[...], h)
    half = HD_A // 2
    for hh in range(H_A):
        r0 = R_QA + hh * HD_A
        x1, x2 = qv_t[r0:r0 + half], qv_t[r0 + half:r0 + HD_A]
        if rope:
            x1, x2 = _rope_rows(x1, x2, cosat_ref[...], sinat_ref[...])
        qat_ref[hh * HD_A:hh * HD_A + half, :] = (x1 * QA_SCALE).astype(BF16)
        qat_ref[hh * HD_A + half:(hh + 1) * HD_A, :] = (x2 * QA_SCALE).astype(BF16)
    vat_ref[...] = qv_t[R_VA:R_END].astype(BF16)
    ka = mm(C_KA, C_ZA)
    if rope:
        lane = lax.broadcasted_iota(jnp.int32, (1, LANES), 1)
        ka = _rope_lanes(ka, cosa_ref[...], sina_ref[...], (lane % HD_A) < half,
                         LANES - half, half)
    ka_ref[...] = ka.astype(BF16)
    sza_ref[...] = _silu(mm(C_ZA, C_QL)).astype(BF16)

    qn = _rms(mm(C_QL, C_KVL), gqa_ref[...]).astype(BF16)
    qb_t = _dot_t(wqbt_ref[...], qn)
    hr = ROPE_B // 2
    for hh in range(H_B):
        r0 = hh * HEAD_PAD
        x1 = qb_t[r0 + NOPE_B:r0 + NOPE_B + hr]
        x2 = qb_t[r0 + NOPE_B + hr:r0 + NOPE_B + ROPE_B]
        if rope:
            x1, x2 = _rope_rows(x1, x2, cosbt_ref[...], sinbt_ref[...])
        qbt_ref[r0:r0 + NOPE_B, :] = (qb_t[r0:r0 + NOPE_B] * QB_SCALE).astype(BF16)
        qbt_ref[r0 + NOPE_B:r0 + NOPE_B + hr, :] = (x1 * QB_SCALE).astype(BF16)
        qbt_ref[r0 + NOPE_B + hr:r0 + NOPE_B + ROPE_B, :] = (x2 * QB_SCALE).astype(BF16)
        qbt_ref[r0 + NOPE_B + ROPE_B:r0 + HEAD_PAD, :] = jnp.zeros(
            (HEAD_PAD - NOPE_B - ROPE_B, x.shape[0]), BF16)
    kvn = _rms(mm(C_KVL, C_KR), gkva_ref[...]).astype(BF16)
    kn = _dot(kvn, wkn_ref[...])
    kr = mm(C_KR, C_ZB)
    if rope:
        kr = _rope_lanes(kr, cosb_ref[...], sinb_ref[...], lane < (NOPE_B + hr),
                         LANES - hr, hr)
    for hh in range(H_B):
        sl = slice(hh * HEAD_PAD, (hh + 1) * HEAD_PAD)
        kb_ref[:, sl] = (kn[:, sl] + kr).astype(BF16)
    vbt_ref[...] = _dot_t(wvt_ref[...], kvn).astype(BF16)
    szb_ref[...] = _silu(mm(C_ZB, C_BC)).astype(BF16)

    bc_ref[...] = mm(C_BC, C_CC).astype(BF16)
    cu_ref[...] = (mm(C_CC, C_UC) * mm(C_UC, C_ZC)).astype(BF16)
    szc_ref[...] = _silu(mm(C_ZC, C_GA)).astype(BF16)

    sga_ref[...] = jax.nn.sigmoid(mm(C_GA, C_GB)).astype(BF16)
    sgb_ref[...] = jax.nn.sigmoid(mm(C_GB, C_GC)).astype(BF16)
    sgc_ref[...] = jax.nn.sigmoid(mm(C_GC, C_END)).astype(BF16)


_FRONT_OUTS = ((True, H_A * HD_A), (False, HKV_A * HD_A), (True, HKV_A * HD_A), (False, W_BRANCH),
               (True, H_B * HEAD_PAD), (False, H_B * HEAD_PAD), (True, H_B * V_B), (False, W_BRANCH),
               (False, W_C), (False, W_C), (False, W_BRANCH),
               (False, D_MODEL), (False, D_MODEL), (False, D_MODEL))


def _front(rows, mod3, mod_row_fn, gpre, w2, wt, gqa, wqbt, gkva, wkn, wvt, tables, tiles_per_seq):
    t_rows = rows.shape[0]
    tm = TM_FRONT
    n_tiles = t_rows // tm
    rope = tables is not None
    const = lambda shape: pl.BlockSpec(shape, lambda i: (0,) * len(shape),
                                       pipeline_mode=pl.Buffered(1))
    in_specs = [
        pl.BlockSpec((tm, D_MODEL), lambda i: (i, 0)),
        pl.BlockSpec((1, 1, D_MODEL), lambda i: (mod_row_fn(i), 0, 0)),
        pl.BlockSpec((1, 1, D_MODEL), lambda i: (mod_row_fn(i), 0, 1)),
        const((1, D_MODEL)),
        const((D_MODEL, C_END)),
        const((R_END, D_MODEL)),
        const((1, Q_LORA)),
        const((H_B * HEAD_PAD, Q_LORA)),
        const((1, KV_LORA)),
        const((KV_LORA, H_B * HEAD_PAD)),
        const((H_B * V_B, KV_LORA)),
    ]
    args = [rows, mod3, mod3, gpre, w2, wt, gqa, wqbt, gkva, wkn, wvt]
    if rope:
        pos = lambda i: i % tiles_per_seq
        in_specs += [pl.BlockSpec((tm, LANES), lambda i: (pos(i), 0))] * 4
        in_specs += [pl.BlockSpec((HD_A // 2, tm), lambda i: (0, pos(i)))] * 2
        in_specs += [pl.BlockSpec((ROPE_B // 2, tm), lambda i: (0, pos(i)))] * 2
        args += list(tables)
    out_specs, out_shape = [], []
    for transposed, width in _FRONT_OUTS:
        if transposed:
            out_specs.append(pl.BlockSpec((width, tm), lambda i: (0, i)))
            out_shape.append(jax.ShapeDtypeStruct((width, t_rows), BF16))
        else:
            out_specs.append(pl.BlockSpec((tm, width), lambda i: (i, 0)))
            out_shape.append(jax.ShapeDtypeStruct((t_rows, width), BF16))
    return pl.pallas_call(
        functools.partial(_front_kernel, rope=rope),
        grid=(n_tiles,),
        in_specs=in_specs,
        out_specs=out_specs,
        out_shape=out_shape,
        compiler_params=pltpu.CompilerParams(
            dimension_semantics=("arbitrary",), vmem_limit_bytes=VMEM_LIMIT),
        name="front_rope" if rope else "front_ctx",
    )(*args)


def _gqa_heads(sink_ref, qt_ref, k_all, vt_all, mask, oT_scr):
    tq = qt_ref.shape[1]
    zeros = jnp.zeros((HD_A, tq), BF16)

    def scores(hh):
        q_t = qt_ref[hh * HD_A:(hh + 1) * HD_A, :]
        w = jnp.concatenate([q_t, zeros] if hh // G_A == 0 else [zeros, q_t], axis=0)
        return _dot(k_all, w)

    s_next = scores(0)
    for hh in range(H_A):
        g = hh // G_A
        s_t = s_next
        if hh + 1 < H_A:
            s_next = scores(hh + 1)
        if mask is not None:
            s_t = jnp.where(mask, s_t, NEG_INF)
        sink = sink_ref[hh] * LOG2E
        m = jnp.maximum(jnp.max(s_t, axis=0, keepdims=True), sink)
        p_t = jnp.exp2(s_t - m)
        den = jnp.sum(p_t, axis=0, keepdims=True) + jnp.exp2(sink - m)
        o_t = _dot(vt_all[g * HD_A:(g + 1) * HD_A, :], p_t.astype(BF16))
        oT_scr[hh * HD_A:(hh + 1) * HD_A, :] = o_t / den


def _mla_tile(qt_ref, k_fn, vt_fn, n_chunks, kc, m_scr, l_scr, acc_scr, s_scr):
    def score_chunk(hh, c, slot, mx):
        s = _dot(k_fn(hh, c), qt_ref[hh * HEAD_PAD:(hh + 1) * HEAD_PAD, :])
        s_scr[slot, c * kc:(c + 1) * kc, :] = s
        cm = jnp.max(s, axis=0, keepdims=True)
        return cm if mx is None else jnp.maximum(mx, cm)

    ones_rows = (lax.broadcasted_iota(jnp.int32, (16, kc), 0) == 0).astype(BF16)
    mx = None
    for c in range(n_chunks):
        mx = score_chunk(0, c, 0, mx)
    for hh in range(H_B):
        slot = hh % 2
        m_prev = m_scr[hh]
        m_new = jnp.maximum(m_prev, mx)
        alpha = jnp.exp2(m_prev - m_new)
        m_scr[hh] = m_new
        pv = None
        mx = None
        for c in range(n_chunks):
            if hh + 1 < H_B:
                mx = score_chunk(hh + 1, c, 1 - slot, mx)
            p = jnp.exp2(s_scr[slot, c * kc:(c + 1) * kc, :] - m_new).astype(BF16)
            d = _dot(jnp.concatenate([vt_fn(hh, c), ones_rows], axis=0), p)
            pv = d if pv is None else pv + d
        l_scr[hh] = alpha * l_scr[hh] + pv[V_B:V_B + 1]
        rows = slice(hh * V_B, (hh + 1) * V_B)
        acc_scr[rows, :] = alpha * acc_scr[rows, :] + pv[:V_B]


def _mla_init(m_scr, l_scr, acc_scr):
    m_scr[...] = jnp.full(m_scr.shape, NEG_INF, F32)
    l_scr[...] = jnp.zeros(l_scr.shape, F32)
    acc_scr[...] = jnp.zeros(acc_scr.shape, F32)


def _mla_finish(o_ref, l_scr, acc_scr):
    for hh in range(H_B):
        rows = slice(hh * V_B, (hh + 1) * V_B)
        acc_scr[rows, :] = acc_scr[rows, :] / l_scr[hh]
    o_ref[...] = acc_scr[...].T.astype(o_ref.dtype)


def _win_kernel(sink_ref, qt_ref, kp_ref, kc_ref, kn_ref, kx_ref, vp_ref, vc_ref, vn_ref, vx_ref,
                o_ref, oT_scr):
    n = pl.program_id(1)
    tq = qt_ref.shape[1]
    n_loc = tq + 2 * WINDOW
    k_all = jnp.concatenate([kp_ref[...], kc_ref[...], kn_ref[...], kx_ref[...]], axis=0)
    vt_all = jnp.concatenate([vp_ref[...], vc_ref[...], vn_ref[...], vx_ref[...]], axis=1)
    nk = k_all.shape[0]
    r = lax.broadcasted_iota(jnp.int32, (nk, tq), 0)
    c = lax.broadcasted_iota(jnp.int32, (nk, tq), 1)
    kabs = n * tq - WINDOW + r
    in_band = (jnp.abs(r - WINDOW - c) <= WINDOW) & (kabs >= 0) & (kabs < SEQ)
    mask = in_band | (r >= n_loc)
    _gqa_heads(sink_ref, qt_ref, k_all, vt_all, mask, oT_scr)
    o_ref[...] = oT_scr[...].T.astype(o_ref.dtype)


def _window_attn(sink, qat, ka, vat, ka_c, vat_c, batch):
    tq = TQ_WIN
    nq = SEQ // tq
    sub = tq // WINDOW
    nw = SEQ // WINDOW
    kv_w = HKV_A * HD_A
    prev_i = lambda b, n: b * nw + jnp.maximum(n * sub - 1, 0)
    next_i = lambda b, n: b * nw + jnp.minimum((n + 1) * sub, nw - 1)
    grid_spec = pltpu.PrefetchScalarGridSpec(
        num_scalar_prefetch=1,
        grid=(batch, nq),
        in_specs=[
            pl.BlockSpec((H_A * HD_A, tq), lambda b, n, s: (0, b * nq + n)),
            pl.BlockSpec((WINDOW, kv_w), lambda b, n, s: (prev_i(b, n), 0)),
            pl.BlockSpec((tq, kv_w), lambda b, n, s: (b * nq + n, 0)),
            pl.BlockSpec((WINDOW, kv_w), lambda b, n, s: (next_i(b, n), 0)),
            pl.BlockSpec((CTX_LEN, kv_w), lambda b, n, s: (b, 0)),
            pl.BlockSpec((kv_w, WINDOW), lambda b, n, s: (0, prev_i(b, n))),
            pl.BlockSpec((kv_w, tq), lambda b, n, s: (0, b * nq + n)),
            pl.BlockSpec((kv_w, WINDOW), lambda b, n, s: (0, next_i(b, n))),
            pl.BlockSpec((kv_w, CTX_LEN), lambda b, n, s: (0, b)),
        ],
        out_specs=pl.BlockSpec((tq, H_A * HD_A), lambda b, n, s: (b * nq + n, 0)),
        scratch_shapes=[pltpu.VMEM((H_A * HD_A, tq), F32)],
    )
    return pl.pallas_call(
        _win_kernel,
        grid_spec=grid_spec,
        out_shape=jax.ShapeDtypeStruct((batch * SEQ, H_A * HD_A), BF16),
        compiler_params=pltpu.CompilerParams(
            dimension_semantics=("arbitrary", "arbitrary"), vmem_limit_bytes=VMEM_LIMIT),
        name="window_attn",
    )(sink, qat, ka, ka, ka, ka_c, vat, vat, vat, vat_c)


def _mla_ctx_tile(qt_ref, kc_ref, vct_ref, m_scr, l_scr, acc_scr, s_scr):
    _mla_tile(qt_ref,
              lambda hh, c: kc_ref[:, hh * HEAD_PAD:(hh + 1) * HEAD_PAD],
              lambda hh, c: vct_ref[hh * V_B:(hh + 1) * V_B, :],
              1, kc_ref.shape[0], m_scr, l_scr, acc_scr, s_scr)


def _mla_kernel(qt_ref, kx_ref, vxt_ref, kc_ref, vct_ref, o_ref, m_scr, l_scr, acc_scr, s_scr):
    _mla_init(m_scr, l_scr, acc_scr)
    n_tiles = kx_ref.shape[0] // TK_MLA
    kc = KC_MLA

    def body(j, carry):
        def rows(c):
            return pl.ds(pl.multiple_of(j * TK_MLA + c * kc, kc), kc)

        _mla_tile(qt_ref,
                  lambda hh, c: kx_ref[rows(c), hh * HEAD_PAD:(hh + 1) * HEAD_PAD],
                  lambda hh, c: vxt_ref[hh * V_B:(hh + 1) * V_B, rows(c)],
                  TK_MLA // kc, kc, m_scr, l_scr, acc_scr, s_scr)
        return carry

    lax.fori_loop(0, n_tiles, body, 0)
    _mla_ctx_tile(qt_ref, kc_ref, vct_ref, m_scr, l_scr, acc_scr, s_scr)
    _mla_finish(o_ref, l_scr, acc_scr)


def _mla_scratch(tq, tk):
    return [pltpu.VMEM((H_B, 1, tq), F32), pltpu.VMEM((H_B, 1, tq), F32),
            pltpu.VMEM((H_B * V_B, tq), F32), pltpu.VMEM((2, tk, tq), F32)]


def _mla_attn(qbt, kb, vbt, kb_c, vbt_c, batch):
    tq = TQ_MLA
    nq = SEQ // tq
    res = lambda shape, fn: pl.BlockSpec(shape, fn, pipeline_mode=pl.Buffered(1))
    return pl.pallas_call(
        _mla_kernel,
        grid=(batch, nq),
        in_specs=[
            pl.BlockSpec((H_B * HEAD_PAD, tq), lambda b, i: (0, b * nq + i)),
            res((SEQ, H_B * HEAD_PAD), lambda b, i: (b, 0)),
            res((H_B * V_B, SEQ), lambda b, i: (0, b)),
            res((CTX_LEN, H_B * HEAD_PAD), lambda b, i: (b, 0)),
            res((H_B * V_B, CTX_LEN), lambda b, i: (0, b)),
        ],
        out_specs=pl.BlockSpec((tq, H_B * V_B), lambda b, i: (b * nq + i, 0)),
        out_shape=jax.ShapeDtypeStruct((batch * SEQ, H_B * V_B), BF16),
        scratch_shapes=_mla_scratch(tq, TK_MLA),
        compiler_params=pltpu.CompilerParams(
            dimension_semantics=("arbitrary", "arbitrary"), vmem_limit_bytes=VMEM_LIMIT),
        name="mla_attn",
    )(qbt, kb, vbt, kb_c, vbt_c)


def _ctx_attn_kernel(sink_ref, qat_ref, ka_ref, vat_ref, qbt_ref, kb_ref, vbt_ref,
                     ya_ref, yb_ref, oT_scr, m_scr, l_scr, acc_scr, s_scr):
    _gqa_heads(sink_ref, qat_ref, ka_ref[...], vat_ref[...], None, oT_scr)
    ya_ref[...] = oT_scr[...].T.astype(ya_ref.dtype)
    _mla_init(m_scr, l_scr, acc_scr)
    _mla_ctx_tile(qbt_ref, kb_ref, vbt_ref, m_scr, l_scr, acc_scr, s_scr)
    _mla_finish(yb_ref, l_scr, acc_scr)


def _ctx_attn(sink, qat_c, ka_c, vat_c, qbt_c, kb_c, vbt_c, batch):
    tok = lambda cols: pl.BlockSpec((CTX_LEN, cols), lambda b, s: (b, 0))
    feat = lambda rows: pl.BlockSpec((rows, CTX_LEN), lambda b, s: (0, b))
    grid_spec = pltpu.PrefetchScalarGridSpec(
        num_scalar_prefetch=1,
        grid=(batch,),
        in_specs=[feat(H_A * HD_A), tok(HKV_A * HD_A), feat(HKV_A * HD_A),
                  feat(H_B * HEAD_PAD), tok(H_B * HEAD_PAD), feat(H_B * V_B)],
        out_specs=[tok(H_A * HD_A), tok(H_B * V_B)],
        scratch_shapes=[pltpu.VMEM((H_A * HD_A, CTX_LEN), F32)] + _mla_scratch(CTX_LEN, CTX_LEN),
    )
    return pl.pallas_call(
        _ctx_attn_kernel,
        grid_spec=grid_spec,
        out_shape=[jax.ShapeDtypeStruct((batch * CTX_LEN, W_BRANCH), BF16)] * 2,
        compiler_params=pltpu.CompilerParams(
            dimension_semantics=("arbitrary",), vmem_limit_bytes=VMEM_LIMIT),
        name="ctx_attn",
    )(sink, qat_c, ka_c, vat_c, qbt_c, kb_c, vbt_c)


def _merge_kernel(ya_ref, yb_ref, sza_ref, szb_ref, szc_ref, bc_ref, cu_ref, cup_ref, cun_ref,
                  sga_ref, sgb_ref, sgc_ref, x_ref, gate_ref, gpost_ref, cw_ref, wbr_ref, wo_ref,
                  o_ref, *, tiles_per_seq):
    i = pl.program_id(0)
    tm = x_ref.shape[0]
    pos = i % tiles_per_seq
    z = cu_ref[...].astype(F32)
    prev_row = jnp.where(pos == 0, 0.0, cup_ref[7:8, :].astype(F32))
    next_row = jnp.where(pos == tiles_per_seq - 1, 0.0, cun_ref[0:1, :].astype(F32))
    row = lax.broadcasted_iota(jnp.int32, z.shape, 0)
    z_m1 = jnp.where(row == 0, prev_row, pltpu.roll(z, 1, 0))
    z_p1 = jnp.where(row == tm - 1, next_row, pltpu.roll(z, tm - 1, 0))
    conv = cw_ref[0:1, :] * z_m1 + cw_ref[1:2, :] * z + cw_ref[2:3, :] * z_p1
    yc = bc_ref[...].astype(F32) * conv

    def branch(y, sz_ref, sg_ref, k):
        t = (y * sz_ref[...].astype(F32)).astype(BF16)
        return sg_ref[...].astype(F32) * _dot(t, wbr_ref[k])

    m = (branch(ya_ref[...].astype(F32), sza_ref, sga_ref, 0)
         + branch(yb_ref[...].astype(F32), szb_ref, sgb_ref, 1)
         + branch(yc, szc_ref, sgc_ref, 2))
    out = _dot(m.astype(BF16), wo_ref[...])
    o_ref[...] = x_ref[...] + gate_ref[0] * _rms(out, gpost_ref[...])


def _merge(ya, yb, fr, resid, mod3, mod_row_fn, gpost, conv_w, wbr, wo, tiles_per_seq):
    (_, _, _, sza, _, _, _, szb, bc, cu, szc, sga, sgb, sgc) = fr
    t_rows = resid.shape[0]
    tm = TM_MERGE
    n_tiles = t_rows // tm
    sub = tm // 8
    n_sub = t_rows // 8
    row = lambda cols: pl.BlockSpec((tm, cols), lambda i: (i, 0))
    const = lambda shape: pl.BlockSpec(shape, lambda i: (0,) * len(shape),
                                       pipeline_mode=pl.Buffered(1))
    in_specs = [
        row(512), row(512), row(512), row(512), row(512), row(512), row(512),
        pl.BlockSpec((8, 512), lambda i: (jnp.maximum(i * sub - 1, 0), 0)),
        pl.BlockSpec((8, 512), lambda i: (jnp.minimum((i + 1) * sub, n_sub - 1), 0)),
        row(1024), row(1024), row(1024), row(1024),
        pl.BlockSpec((1, 1, D_MODEL), lambda i: (mod_row_fn(i), 0, 2)),
        const((1, D_MODEL)),
        const((3, W_C)),
        const((3, W_BRANCH, D_MODEL)),
        const((D_MODEL, D_MODEL)),
    ]
    return pl.pallas_call(
        functools.partial(_merge_kernel, tiles_per_seq=tiles_per_seq),
        grid=(n_tiles,),
        in_specs=in_specs,
        out_specs=pl.BlockSpec((tm, D_MODEL), lambda i: (i, 0)),
        out_shape=jax.ShapeDtypeStruct((t_rows, D_MODEL), F32),
        compiler_params=pltpu.CompilerParams(
            dimension_semantics=("arbitrary",), vmem_limit_bytes=VMEM_LIMIT),
        name="merge",
    )(ya, yb, sza, szb, szc, bc, cu, cu, cu, sga, sgb, sgc, resid, mod3, gpost, conv_w, wbr, wo)


def _pack_w_in(w):
    offs = np.concatenate([[0], np.cumsum(IN_SIZES)])
    seg = [w[:, offs[k]:offs[k + 1]] for k in range(len(IN_SIZES))]
    qa, ka, va, za, ql, kvl, kr, zb, bc, cc, uc, zc, ga, gb, gc = seg
    d = w.shape[0]
    krp = jnp.concatenate([jnp.zeros((d, NOPE_B), w.dtype), kr,
                           jnp.zeros((d, HEAD_PAD - NOPE_B - ROPE_B), w.dtype)], axis=1)
    w2 = jnp.concatenate([ka, za, ql, kvl, krp, zb, bc, cc, uc, zc, ga, gb, gc], axis=1)
    wt = jnp.concatenate([qa, va], axis=1).T
    return w2.astype(BF16), wt.astype(BF16)


def _pack_w_qb(w):
    w3 = w.reshape(Q_LORA, H_B, NOPE_B + ROPE_B)
    w3 = jnp.pad(w3, ((0, 0), (0, 0), (0, HEAD_PAD - NOPE_B - ROPE_B)))
    return w3.reshape(Q_LORA, H_B * HEAD_PAD).T.astype(BF16)


def _pack_w_kvb(w):
    w3 = w.reshape(KV_LORA, H_B, NOPE_B + V_B)
    wkn = jnp.pad(w3[..., :NOPE_B], ((0, 0), (0, 0), (0, HEAD_PAD - NOPE_B)))
    wv = w3[..., NOPE_B:]
    return (wkn.reshape(KV_LORA, H_B * HEAD_PAD).astype(BF16),
            wv.reshape(KV_LORA, H_B * V_B).T.astype(BF16))


def _rope_tables(seq):
    t = jnp.arange(seq)
    row = (t // GRID_W).astype(F32)
    col = (t % GRID_W).astype(F32)

    def angles(rot_dim):
        axis_dim = rot_dim // 2
        inv = ROPE_BASE ** (-jnp.arange(0, axis_dim, 2, dtype=F32) / axis_dim)
        return jnp.concatenate([row[:, None] * inv, col[:, None] * inv], axis=-1)

    a = angles(HD_A)
    ca, sa = jnp.cos(a), jnp.sin(a)
    cos_a = jnp.tile(jnp.concatenate([ca, ca], axis=-1), (1, LANES // HD_A))
    sin_a = jnp.tile(jnp.concatenate([-sa, sa], axis=-1), (1, LANES // HD_A))
    b = angles(ROPE_B)
    cb, sb = jnp.cos(b), jnp.sin(b)
    tail = HEAD_PAD - NOPE_B - ROPE_B
    cos_b = jnp.concatenate([jnp.ones((seq, NOPE_B), F32), cb, cb, jnp.ones((seq, tail), F32)], axis=-1)
    sin_b = jnp.concatenate([jnp.zeros((seq, NOPE_B), F32), -sb, sb, jnp.zeros((seq, tail), F32)], axis=-1)
    return cos_a, sin_a, cos_b, sin_b, ca.T, sa.T, cb.T, sb.T


def kernel(x, c, ctx, c_ctx, w_mod, b_mod, g_pre, g_post, w_in, sink, g_qa, w_qb, g_kva, w_kvb,
           conv_w, w_branch, w_o):
    batch, seq, d = x.shape
    assert (seq, d) == (SEQ, D_MODEL) and ctx.shape == (batch, CTX_LEN, D_MODEL)
    assert batch <= 6
    depth = w_mod.shape[0]

    c_rows = jnp.concatenate([c, c_ctx[None, :], jnp.zeros((8 - batch - 1, d), F32)], axis=0)
    mod_all = _modulation(c_rows, w_mod, b_mod)
    tables = _rope_tables(seq)

    x2 = x.reshape(batch * seq, d)
    ctx2 = ctx.reshape(batch * CTX_LEN, d)
    x_tiles_front = seq // TM_FRONT
    x_tiles_merge = seq // TM_MERGE

    for i in range(depth):
        mod3 = mod_all[i].reshape(8, 1, 3 * d)
        w2, wt = _pack_w_in(w_in[i])
        wqbt = _pack_w_qb(w_qb[i])
        wkn, wvt = _pack_w_kvb(w_kvb[i])
        gpre = g_pre[i].reshape(1, d)
        gqa = g_qa[i].reshape(1, Q_LORA)
        gkva = g_kva[i].reshape(1, KV_LORA)
        gpost = g_post[i].reshape(1, d)
        wbr = w_branch[i].astype(BF16)
        wo = w_o[i].astype(BF16)

        fx = _front(x2, mod3, lambda t: t // x_tiles_front, gpre, w2, wt, gqa, wqbt, gkva, wkn,
                    wvt, tables, x_tiles_front)
        fc = _front(ctx2, mod3, lambda t: batch, gpre, w2, wt, gqa, wqbt, gkva, wkn, wvt, None, 1)
        qat, ka, vat, _, qbt, kb, vbt = fx[:7]
        qat_c, ka_c, vat_c, _, qbt_c, kb_c, vbt_c = fc[:7]

        ya = _window_attn(sink[i], qat, ka, vat, ka_c, vat_c, batch)
        yb = _mla_attn(qbt, kb, vbt, kb_c, vbt_c, batch)
        x2_new = _merge(ya, yb, fx, x2, mod3, lambda t: t // x_tiles_merge, gpost, conv_w[i],
                        wbr, wo, x_tiles_merge)
        if i < depth - 1:
            ya_c, yb_c = _ctx_attn(sink[i], qat_c, ka_c, vat_c, qbt_c, kb_c, vbt_c, batch)
            ctx2 = _merge(ya_c, yb_c, fc, ctx2, mod3, lambda t: batch, gpost, conv_w[i],
                          wbr, wo, CTX_LEN // TM_MERGE)
        x2 = x2_new
    return x2.reshape(batch, seq, d)
```

```python
import functools

import jax
import jax.numpy as jnp
import numpy as np
from jax import lax
from jax.experimental import pallas as pl
from jax.experimental.pallas import tpu as pltpu

F32 = jnp.float32
BF16 = jnp.bfloat16

D_MODEL = 1024
SEQ = 8192
DEPTH = 2
CTX_LEN = 256
GRID_W = 64
WINDOW = 128
ROPE_BASE = 10000.0
EPS = 1e-6
NEG_INF = -1e30
LOG2E = float(np.log2(np.e))

H_A = 8
HKV_A = 2
G_A = H_A // HKV_A
HD_A = 64
QA_SCALE = HD_A ** -0.5 * LOG2E

H_B = 8
Q_LORA = 384
KV_LORA = 256
NOPE_B = 64
ROPE_B = 32
V_B = 64
QB_SCALE = (NOPE_B + ROPE_B) ** -0.5 * LOG2E

W_C = 512
W_BRANCH = 512

IN_SIZES = (H_A * HD_A, HKV_A * HD_A, HKV_A * HD_A, W_BRANCH,
            Q_LORA, KV_LORA, ROPE_B, W_BRANCH,
            W_C, W_C, W_C, W_BRANCH,
            D_MODEL, D_MODEL, D_MODEL)

LANES = 128
HEAD_PAD = 128

C_KA = 0
C_ZA = 128
C_QL = 640
C_KVL = 1024
C_KR = 1280
C_ZB = 1408
C_BC = 1920
C_CC = 2432
C_UC = 2944
C_ZC = 3456
C_GA = 3968
C_GB = 4992
C_GC = 6016
C_END = 7040
R_QA = 0
R_VA = H_A * HD_A
R_END = R_VA + HKV_A * HD_A

VMEM_LIMIT = 56 * 1024 * 1024

TM_FRONT = 512
TM_MERGE = 256
TQ_WIN = 256
TQ_MLA = 512
TK_MLA = 4096
KC_MLA = 512
KC_WIN = 256
SCORE_PAD = LANES


def _silu(v):
    return v * jax.nn.sigmoid(v)


def _rms(v, g):
    return v * lax.rsqrt(jnp.mean(v * v, axis=-1, keepdims=True) + EPS) * g


def _dot(a, b):
    return jnp.dot(a, b, preferred_element_type=F32)


def _dot_t(a, b):
    return lax.dot_general(a, b, (((1,), (1,)), ((), ())), preferred_element_type=F32)


def _mod_kernel(c_ref, w_ref, b_ref, o_ref):
    sc = _silu(c_ref[...])
    o_ref[0] = jnp.dot(sc, w_ref[0], preferred_element_type=F32,
                       precision=lax.Precision.HIGHEST) + b_ref[0]


def _modulation(c_rows, w_mod, b_mod):
    n_chunk = 3
    return pl.pallas_call(
        _mod_kernel,
        grid=(DEPTH, n_chunk),
        in_specs=[
            pl.BlockSpec((8, D_MODEL), lambda l, j: (0, 0)),
            pl.BlockSpec((1, D_MODEL, D_MODEL), lambda l, j: (l, 0, j)),
            pl.BlockSpec((1, 1, D_MODEL), lambda l, j: (l, 0, j)),
        ],
        out_specs=pl.BlockSpec((1, 8, D_MODEL), lambda l, j: (l, 0, j)),
        out_shape=jax.ShapeDtypeStruct((DEPTH, 8, 3 * D_MODEL), F32),
        compiler_params=pltpu.CompilerParams(
            dimension_semantics=("arbitrary", "arbitrary"), vmem_limit_bytes=VMEM_LIMIT),
        name="modulation",
    )(c_rows, w_mod, b_mod.reshape(DEPTH, 1, 3 * D_MODEL))


def _rope_lanes(v, cos, sin, first_half, up_shift, dn_shift):
    rot = jnp.where(first_half, pltpu.roll(v, up_shift, 1), pltpu.roll(v, dn_shift, 1))
    return v * cos + rot * sin


def _rope_rows(x1, x2, cos, sin):
    return x1 * cos - x2 * sin, x2 * cos + x1 * sin


def _front_kernel(x_ref, shift_ref, scale_ref, gpre_ref, w_ref, wt_ref, gqa_ref, wqbt_ref,
                  gkva_ref, wkn_ref, wvt_ref, *rest, rope):
    if rope:
        (cosa_ref, sina_ref, cosb_ref, sinb_ref,
         cosat_ref, sinat_ref, cosbt_ref, sinbt_ref) = rest[:8]
        outs = rest[8:]
    else:
        outs = rest
    (qat_ref, ka_ref, vat_ref, sza_ref, qbt_ref, kb_ref, vbt_ref, szb_ref,
     bc_ref, cu_ref, szc_ref, sga_ref, sgb_ref, sgc_ref) = outs

    x = x_ref[...]
    h = _rms(x, gpre_ref[...]) * (1.0 + scale_ref[0]) + shift_ref[0]
    h = h.astype(BF16)

    def mm(a, b):
        return _dot(h, w_ref[:, a:b])

    qv_t = _dot_t(wt_ref[...], h)
    half = HD_A // 2
    for hh in range(H_A):
        r0 = R_QA + hh * HD_A
        x1, x2 = qv_t[r0:r0 + half], qv_t[r0 + half:r0 + HD_A]
        if rope:
            x1, x2 = _rope_rows(x1, x2, cosat_ref[...], sinat_ref[...])
        qat_ref[hh * HD_A:hh * HD_A + half, :] = (x1 * QA_SCALE).astype(BF16)
        qat_ref[hh * HD_A + half:(hh + 1) * HD_A, :] = (x2 * QA_SCALE).astype(BF16)
    vat_ref[...] = qv_t[R_VA:R_END].astype(BF16)
    ka = mm(C_KA, C_ZA)
    if rope:
        lane = lax.broadcasted_iota(jnp.int32, (1, LANES), 1)
        ka = _rope_lanes(ka, cosa_ref[...], sina_ref[...], (lane % HD_A) < half,
                         LANES - half, half)
    ka_ref[...] = ka.astype(BF16)
    sza_ref[...] = _silu(mm(C_ZA, C_QL)).astype(BF16)

    qn = _rms(mm(C_QL, C_KVL), gqa_ref[...]).astype(BF16)
    qb_t = _dot_t(wqbt_ref[...], qn)
    hr = ROPE_B // 2
    for hh in range(H_B):
        r0 = hh * HEAD_PAD
        x1 = qb_t[r0 + NOPE_B:r0 + NOPE_B + hr]
        x2 = qb_t[r0 + NOPE_B + hr:r0 + NOPE_B + ROPE_B]
        if rope:
            x1, x2 = _rope_rows(x1, x2, cosbt_ref[...], sinbt_ref[...])
        qbt_ref[r0:r0 + NOPE_B, :] = (qb_t[r0:r0 + NOPE_B] * QB_SCALE).astype(BF16)
        qbt_ref[r0 + NOPE_B:r0 + NOPE_B + hr, :] = (x1 * QB_SCALE).astype(BF16)
        qbt_ref[r0 + NOPE_B + hr:r0 + NOPE_B + ROPE_B, :] = (x2 * QB_SCALE).astype(BF16)
        qbt_ref[r0 + NOPE_B + ROPE_B:r0 + HEAD_PAD, :] = jnp.zeros(
            (HEAD_PAD - NOPE_B - ROPE_B, x.shape[0]), BF16)
    kvn = _rms(mm(C_KVL, C_KR), gkva_ref[...]).astype(BF16)
    kn = _dot(kvn, wkn_ref[...])
    kr = mm(C_KR, C_ZB)
    if rope:
        kr = _rope_lanes(kr, cosb_ref[...], sinb_ref[...], lane < (NOPE_B + hr),
                         LANES - hr, hr)
    for hh in range(H_B):
        sl = slice(hh * HEAD_PAD, (hh + 1) * HEAD_PAD)
        kb_ref[:, sl] = (kn[:, sl] + kr).astype(BF16)
    vbt_ref[...] = _dot_t(wvt_ref[...], kvn).astype(BF16)
    szb_ref[...] = _silu(mm(C_ZB, C_BC)).astype(BF16)

    bc_ref[...] = mm(C_BC, C_CC).astype(BF16)
    cu_ref[...] = (mm(C_CC, C_UC) * mm(C_UC, C_ZC)).astype(BF16)
    szc_ref[...] = _silu(mm(C_ZC, C_GA)).astype(BF16)

    sga_ref[...] = jax.nn.sigmoid(mm(C_GA, C_GB)).astype(BF16)
    sgb_ref[...] = jax.nn.sigmoid(mm(C_GB, C_GC)).astype(BF16)
    sgc_ref[...] = jax.nn.sigmoid(mm(C_GC, C_END)).astype(BF16)


_FRONT_OUTS = ((True, H_A * HD_A), (False, HKV_A * HD_A), (True, HKV_A * HD_A), (False, W_BRANCH),
               (True, H_B * HEAD_PAD), (False, H_B * HEAD_PAD), (True, H_B * V_B), (False, W_BRANCH),
               (False, W_C), (False, W_C), (False, W_BRANCH),
               (False, D_MODEL), (False, D_MODEL), (False, D_MODEL))


def _front(rows, mod3, mod_row_fn, gpre, w2, wt, gqa, wqbt, gkva, wkn, wvt, tables, tiles_per_seq):
    t_rows = rows.shape[0]
    tm = TM_FRONT
    n_tiles = t_rows // tm
    rope = tables is not None
    const = lambda shape: pl.BlockSpec(shape, lambda i: (0,) * len(shape),
                                       pipeline_mode=pl.Buffered(1))
    in_specs = [
        pl.BlockSpec((tm, D_MODEL), lambda i: (i, 0)),
        pl.BlockSpec((1, 1, D_MODEL), lambda i: (mod_row_fn(i), 0, 0)),
        pl.BlockSpec((1, 1, D_MODEL), lambda i: (mod_row_fn(i), 0, 1)),
        const((1, D_MODEL)),
        const((D_MODEL, C_END)),
        const((R_END, D_MODEL)),
        const((1, Q_LORA)),
        const((H_B * HEAD_PAD, Q_LORA)),
        const((1, KV_LORA)),
        const((KV_LORA, H_B * HEAD_PAD)),
        const((H_B * V_B, KV_LORA)),
    ]
    args = [rows, mod3, mod3, gpre, w2, wt, gqa, wqbt, gkva, wkn, wvt]
    if rope:
        pos = lambda i: i % tiles_per_seq
        in_specs += [pl.BlockSpec((tm, LANES), lambda i: (pos(i), 0))] * 4
        in_specs += [pl.BlockSpec((HD_A // 2, tm), lambda i: (0, pos(i)))] * 2
        in_specs += [pl.BlockSpec((ROPE_B // 2, tm), lambda i: (0, pos(i)))] * 2
        args += list(tables)
    out_specs, out_shape = [], []
    for transposed, width in _FRONT_OUTS:
        if transposed:
            out_specs.append(pl.BlockSpec((width, tm), lambda i: (0, i)))
            out_shape.append(jax.ShapeDtypeStruct((width, t_rows), BF16))
        else:
            out_specs.append(pl.BlockSpec((tm, width), lambda i: (i, 0)))
            out_shape.append(jax.ShapeDtypeStruct((t_rows, width), BF16))
    return pl.pallas_call(
        functools.partial(_front_kernel, rope=rope),
        grid=(n_tiles,),
        in_specs=in_specs,
        out_specs=out_specs,
        out_shape=out_shape,
        compiler_params=pltpu.CompilerParams(
            dimension_semantics=("arbitrary",), vmem_limit_bytes=VMEM_LIMIT),
        name="front_rope" if rope else "front_ctx",
    )(*args)


def _score_scratch(nk, tq):
    return pltpu.VMEM((2, nk, tq + SCORE_PAD), F32)


def _ones_rows(kc):
    return (lax.broadcasted_iota(jnp.int32, (16, kc), 0) == 0).astype(BF16)


def _gqa_heads(sink_ref, qt_ref, k_chunks, vt_chunks, masks, s_scr, oT_scr):
    tq = qt_ref.shape[1]
    kc = k_chunks[0].shape[0]
    n_chunks = len(k_chunks)
    zeros = jnp.zeros((HD_A, tq), BF16)
    ones_rows = _ones_rows(kc)

    def weights(hh):
        q_t = qt_ref[hh * HD_A:(hh + 1) * HD_A, :]
        return jnp.concatenate([q_t, zeros] if hh // G_A == 0 else [zeros, q_t], axis=0)

    def score_chunk(w, c, slot, mx):
        s = _dot(k_chunks[c], w)
        if masks[c] is not None:
            s = jnp.where(masks[c], s, NEG_INF)
        s_scr[slot, c * kc:(c + 1) * kc, 0:tq] = s
        cm = jnp.max(s, axis=0, keepdims=True)
        return cm if mx is None else jnp.maximum(mx, cm)

    mx = None
    w = weights(0)
    for c in range(n_chunks):
        mx = score_chunk(w, c, 0, mx)
    for hh in range(H_A):
        g = hh // G_A
        slot = hh % 2
        sink = sink_ref[hh] * LOG2E
        m = jnp.maximum(mx, sink)
        pv = None
        mx = None
        if hh + 1 < H_A:
            w = weights(hh + 1)
        for c in range(n_chunks):
            if hh + 1 < H_A:
                mx = score_chunk(w, c, 1 - slot, mx)
            p = jnp.exp2(s_scr[slot, c * kc:(c + 1) * kc, 0:tq] - m).astype(BF16)
            vt = jnp.concatenate([vt_chunks[c][g * HD_A:(g + 1) * HD_A, :], ones_rows], axis=0)
            d = _dot(vt, p)
            pv = d if pv is None else pv + d
        den = pv[HD_A:HD_A + 1] + jnp.exp2(sink - m)
        oT_scr[hh * HD_A:(hh + 1) * HD_A, :] = pv[:HD_A] / den


def _mla_tile(qt_ref, k_fn, vt_fn, n_chunks, kc, m_scr, l_scr, acc_scr, s_scr):
    tq = qt_ref.shape[1]

    def score_chunk(hh, c, slot, mx):
        s = _dot(k_fn(hh, c), qt_ref[hh * HEAD_PAD:(hh + 1) * HEAD_PAD, :])
        s_scr[slot, c * kc:(c + 1) * kc, 0:tq] = s
        cm = jnp.max(s, axis=0, keepdims=True)
        return cm if mx is None else jnp.maximum(mx, cm)

    ones_rows = _ones_rows(kc)
    mx = None
    for c in range(n_chunks):
        mx = score_chunk(0, c, 0, mx)
    for hh in range(H_B):
        slot = hh % 2
        m_prev = m_scr[hh]
        m_new = jnp.maximum(m_prev, mx)
        alpha = jnp.exp2(m_prev - m_new)
        m_scr[hh] = m_new
        pv = None
        mx = None
        for c in range(n_chunks):
            if hh + 1 < H_B:
                mx = score_chunk(hh + 1, c, 1 - slot, mx)
            p = jnp.exp2(s_scr[slot, c * kc:(c + 1) * kc, 0:tq] - m_new).astype(BF16)
            d = _dot(jnp.concatenate([vt_fn(hh, c), ones_rows], axis=0), p)
            pv = d if pv is None else pv + d
        l_scr[hh] = alpha * l_scr[hh] + pv[V_B:V_B + 1]
        rows = slice(hh * V_B, (hh + 1) * V_B)
        acc_scr[rows, :] = alpha * acc_scr[rows, :] + pv[:V_B]


def _mla_init(m_scr, l_scr, acc_scr):
    m_scr[...] = jnp.full(m_scr.shape, NEG_INF, F32)
    l_scr[...] = jnp.zeros(l_scr.shape, F32)
    acc_scr[...] = jnp.zeros(acc_scr.shape, F32)


def _mla_finish(o_ref, l_scr, acc_scr):
    for hh in range(H_B):
        rows = slice(hh * V_B, (hh + 1) * V_B)
        acc_scr[rows, :] = acc_scr[rows, :] / l_scr[hh]
    o_ref[...] = acc_scr[...].T.astype(o_ref.dtype)


def _win_kernel(sink_ref, qt_ref, kp_ref, kc_ref, kn_ref, kx_ref, vp_ref, vc_ref, vn_ref, vx_ref,
                o_ref, oT_scr, s_scr):
    n = pl.program_id(1)
    tq = qt_ref.shape[1]
    kc = KC_WIN
    n_loc = (tq + 2 * WINDOW) // kc
    k_loc = jnp.concatenate([kp_ref[...], kc_ref[...], kn_ref[...]], axis=0)
    vt_loc = jnp.concatenate([vp_ref[...], vc_ref[...], vn_ref[...]], axis=1)
    k_chunks = [k_loc[c * kc:(c + 1) * kc, :] for c in range(n_loc)] + [kx_ref[...]]
    vt_chunks = [vt_loc[:, c * kc:(c + 1) * kc] for c in range(n_loc)] + [vx_ref[...]]
    col = lax.broadcasted_iota(jnp.int32, (kc, tq), 1)
    masks = []
    for c in range(n_loc):
        r = lax.broadcasted_iota(jnp.int32, (kc, tq), 0) + c * kc
        kabs = n * tq - WINDOW + r
        masks.append((jnp.abs(r - WINDOW - col) <= WINDOW) & (kabs >= 0) & (kabs < SEQ))
    masks.append(None)
    _gqa_heads(sink_ref, qt_ref, k_chunks, vt_chunks, masks, s_scr, oT_scr)
    o_ref[...] = oT_scr[...].T.astype(o_ref.dtype)


def _window_attn(sink, qat, ka, vat, ka_c, vat_c, batch):
    tq = TQ_WIN
    nq = SEQ // tq
    sub = tq // WINDOW
    nw = SEQ // WINDOW
    kv_w = HKV_A * HD_A
    prev_i = lambda b, n: b * nw + jnp.maximum(n * sub - 1, 0)
    next_i = lambda b, n: b * nw + jnp.minimum((n + 1) * sub, nw - 1)
    grid_spec = pltpu.PrefetchScalarGridSpec(
        num_scalar_prefetch=1,
        grid=(batch, nq),
        in_specs=[
            pl.BlockSpec((H_A * HD_A, tq), lambda b, n, s: (0, b * nq + n)),
            pl.BlockSpec((WINDOW, kv_w), lambda b, n, s: (prev_i(b, n), 0)),
            pl.BlockSpec((tq, kv_w), lambda b, n, s: (b * nq + n, 0)),
            pl.BlockSpec((WINDOW, kv_w), lambda b, n, s: (next_i(b, n), 0)),
            pl.BlockSpec((CTX_LEN, kv_w), lambda b, n, s: (b, 0)),
            pl.BlockSpec((kv_w, WINDOW), lambda b, n, s: (0, prev_i(b, n))),
            pl.BlockSpec((kv_w, tq), lambda b, n, s: (0, b * nq + n)),
            pl.BlockSpec((kv_w, WINDOW), lambda b, n, s: (0, next_i(b, n))),
            pl.BlockSpec((kv_w, CTX_LEN), lambda b, n, s: (0, b)),
        ],
        out_specs=pl.BlockSpec((tq, H_A * HD_A), lambda b, n, s: (b * nq + n, 0)),
        scratch_shapes=[pltpu.VMEM((H_A * HD_A, tq), F32), _score_scratch(tq + 2 * WINDOW + CTX_LEN, tq)],
    )
    return pl.pallas_call(
        _win_kernel,
        grid_spec=grid_spec,
        out_shape=jax.ShapeDtypeStruct((batch * SEQ, H_A * HD_A), BF16),
        compiler_params=pltpu.CompilerParams(
            dimension_semantics=("arbitrary", "arbitrary"), vmem_limit_bytes=VMEM_LIMIT),
        name="window_attn",
    )(sink, qat, ka, ka, ka, ka_c, vat, vat, vat, vat_c)


def _mla_ctx_tile(qt_ref, kc_ref, vct_ref, m_scr, l_scr, acc_scr, s_scr):
    _mla_tile(qt_ref,
              lambda hh, c: kc_ref[:, hh * HEAD_PAD:(hh + 1) * HEAD_PAD],
              lambda hh, c: vct_ref[hh * V_B:(hh + 1) * V_B, :],
              1, kc_ref.shape[0], m_scr, l_scr, acc_scr, s_scr)


def _mla_kernel(qt_ref, kx_ref, vxt_ref, kc_ref, vct_ref, o_ref, m_scr, l_scr, acc_scr, s_scr):
    _mla_init(m_scr, l_scr, acc_scr)
    n_tiles = kx_ref.shape[0] // TK_MLA
    kc = KC_MLA

    def body(j, carry):
        def rows(c):
            return pl.ds(pl.multiple_of(j * TK_MLA + c * kc, kc), kc)

        _mla_tile(qt_ref,
                  lambda hh, c: kx_ref[rows(c), hh * HEAD_PAD:(hh + 1) * HEAD_PAD],
                  lambda hh, c: vxt_ref[hh * V_B:(hh + 1) * V_B, rows(c)],
                  TK_MLA // kc, kc, m_scr, l_scr, acc_scr, s_scr)
        return carry

    lax.fori_loop(0, n_tiles, body, 0)
    _mla_ctx_tile(qt_ref, kc_ref, vct_ref, m_scr, l_scr, acc_scr, s_scr)
    _mla_finish(o_ref, l_scr, acc_scr)


def _mla_scratch(tq, tk):
    return [pltpu.VMEM((H_B, 1, tq), F32), pltpu.VMEM((H_B, 1, tq), F32),
            pltpu.VMEM((H_B * V_B, tq), F32), _score_scratch(tk, tq)]


def _mla_attn(qbt, kb, vbt, kb_c, vbt_c, batch):
    tq = TQ_MLA
    nq = SEQ // tq
    res = lambda shape, fn: pl.BlockSpec(shape, fn, pipeline_mode=pl.Buffered(1))
    return pl.pallas_call(
        _mla_kernel,
        grid=(batch, nq),
        in_specs=[
            pl.BlockSpec((H_B * HEAD_PAD, tq), lambda b, i: (0, b * nq + i)),
            res((SEQ, H_B * HEAD_PAD), lambda b, i: (b, 0)),
            res((H_B * V_B, SEQ), lambda b, i: (0, b)),
            res((CTX_LEN, H_B * HEAD_PAD), lambda b, i: (b, 0)),
            res((H_B * V_B, CTX_LEN), lambda b, i: (0, b)),
        ],
        out_specs=pl.BlockSpec((tq, H_B * V_B), lambda b, i: (b * nq + i, 0)),
        out_shape=jax.ShapeDtypeStruct((batch * SEQ, H_B * V_B), BF16),
        scratch_shapes=_mla_scratch(tq, TK_MLA),
        compiler_params=pltpu.CompilerParams(
            dimension_semantics=("arbitrary", "arbitrary"), vmem_limit_bytes=VMEM_LIMIT),
        name="mla_attn",
    )(qbt, kb, vbt, kb_c, vbt_c)


def _ctx_attn_kernel(sink_ref, qat_ref, ka_ref, vat_ref, qbt_ref, kb_ref, vbt_ref,
                     ya_ref, yb_ref, oT_scr, m_scr, l_scr, acc_scr, s_scr):
    _gqa_heads(sink_ref, qat_ref, [ka_ref[...]], [vat_ref[...]], [None], s_scr, oT_scr)
    ya_ref[...] = oT_scr[...].T.astype(ya_ref.dtype)
    _mla_init(m_scr, l_scr, acc_scr)
    _mla_ctx_tile(qbt_ref, kb_ref, vbt_ref, m_scr, l_scr, acc_scr, s_scr)
    _mla_finish(yb_ref, l_scr, acc_scr)


def _ctx_attn(sink, qat_c, ka_c, vat_c, qbt_c, kb_c, vbt_c, batch):
    tok = lambda cols: pl.BlockSpec((CTX_LEN, cols), lambda b, s: (b, 0))
    feat = lambda rows: pl.BlockSpec((rows, CTX_LEN), lambda b, s: (0, b))
    grid_spec = pltpu.PrefetchScalarGridSpec(
        num_scalar_prefetch=1,
        grid=(batch,),
        in_specs=[feat(H_A * HD_A), tok(HKV_A * HD_A), feat(HKV_A * HD_A),
                  feat(H_B * HEAD_PAD), tok(H_B * HEAD_PAD), feat(H_B * V_B)],
        out_specs=[tok(H_A * HD_A), tok(H_B * V_B)],
        scratch_shapes=[pltpu.VMEM((H_A * HD_A, CTX_LEN), F32)] + _mla_scratch(CTX_LEN, CTX_LEN),
    )
    return pl.pallas_call(
        _ctx_attn_kernel,
        grid_spec=grid_spec,
        out_shape=[jax.ShapeDtypeStruct((batch * CTX_LEN, W_BRANCH), BF16)] * 2,
        compiler_params=pltpu.CompilerParams(
            dimension_semantics=("arbitrary",), vmem_limit_bytes=VMEM_LIMIT),
        name="ctx_attn",
    )(sink, qat_c, ka_c, vat_c, qbt_c, kb_c, vbt_c)


def _merge_kernel(ya_ref, yb_ref, sza_ref, szb_ref, szc_ref, bc_ref, cu_ref, cup_ref, cun_ref,
                  sga_ref, sgb_ref, sgc_ref, x_ref, gate_ref, gpost_ref, cw_ref, wbr_ref, wo_ref,
                  o_ref, *, tiles_per_seq):
    i = pl.program_id(0)
    tm = x_ref.shape[0]
    pos = i % tiles_per_seq
    z = cu_ref[...].astype(F32)
    prev_row = jnp.where(pos == 0, 0.0, cup_ref[7:8, :].astype(F32))
    next_row = jnp.where(pos == tiles_per_seq - 1, 0.0, cun_ref[0:1, :].astype(F32))
    row = lax.broadcasted_iota(jnp.int32, z.shape, 0)
    z_m1 = jnp.where(row == 0, prev_row, pltpu.roll(z, 1, 0))
    z_p1 = jnp.where(row == tm - 1, next_row, pltpu.roll(z, tm - 1, 0))
    conv = cw_ref[0:1, :] * z_m1 + cw_ref[1:2, :] * z + cw_ref[2:3, :] * z_p1
    yc = bc_ref[...].astype(F32) * conv

    def branch(y, sz_ref, sg_ref, k):
        t = (y * sz_ref[...].astype(F32)).astype(BF16)
        return sg_ref[...].astype(F32) * _dot(t, wbr_ref[k])

    m = (branch(ya_ref[...].astype(F32), sza_ref, sga_ref, 0)
         + branch(yb_ref[...].astype(F32), szb_ref, sgb_ref, 1)
         + branch(yc, szc_ref, sgc_ref, 2))
    out = _dot(m.astype(BF16), wo_ref[...])
    o_ref[...] = x_ref[...] + gate_ref[0] * _rms(out, gpost_ref[...])


def _merge(ya, yb, fr, resid, mod3, mod_row_fn, gpost, conv_w, wbr, wo, tiles_per_seq):
    (_, _, _, sza, _, _, _, szb, bc, cu, szc, sga, sgb, sgc) = fr
    t_rows = resid.shape[0]
    tm = TM_MERGE
    n_tiles = t_rows // tm
    sub = tm // 8
    n_sub = t_rows // 8
    row = lambda cols: pl.BlockSpec((tm, cols), lambda i: (i, 0))
    const = lambda shape: pl.BlockSpec(shape, lambda i: (0,) * len(shape),
                                       pipeline_mode=pl.Buffered(1))
    in_specs = [
        row(512), row(512), row(512), row(512), row(512), row(512), row(512),
        pl.BlockSpec((8, 512), lambda i: (jnp.maximum(i * sub - 1, 0), 0)),
        pl.BlockSpec((8, 512), lambda i: (jnp.minimum((i + 1) * sub, n_sub - 1), 0)),
        row(1024), row(1024), row(1024), row(1024),
        pl.BlockSpec((1, 1, D_MODEL), lambda i: (mod_row_fn(i), 0, 2)),
        const((1, D_MODEL)),
        const((3, W_C)),
        const((3, W_BRANCH, D_MODEL)),
        const((D_MODEL, D_MODEL)),
    ]
    return pl.pallas_call(
        functools.partial(_merge_kernel, tiles_per_seq=tiles_per_seq),
        grid=(n_tiles,),
        in_specs=in_specs,
        out_specs=pl.BlockSpec((tm, D_MODEL), lambda i: (i, 0)),
        out_shape=jax.ShapeDtypeStruct((t_rows, D_MODEL), F32),
        compiler_params=pltpu.CompilerParams(
            dimension_semantics=("arbitrary",), vmem_limit_bytes=VMEM_LIMIT),
        name="merge",
    )(ya, yb, sza, szb, szc, bc, cu, cu, cu, sga, sgb, sgc, resid, mod3, gpost, conv_w, wbr, wo)


def _pack_w_in(w):
    offs = np.concatenate([[0], np.cumsum(IN_SIZES)])
    seg = [w[:, offs[k]:offs[k + 1]] for k in range(len(IN_SIZES))]
    qa, ka, va, za, ql, kvl, kr, zb, bc, cc, uc, zc, ga, gb, gc = seg
    d = w.shape[0]
    krp = jnp.concatenate([jnp.zeros((d, NOPE_B), w.dtype), kr,
                           jnp.zeros((d, HEAD_PAD - NOPE_B - ROPE_B), w.dtype)], axis=1)
    w2 = jnp.concatenate([ka, za, ql, kvl, krp, zb, bc, cc, uc, zc, ga, gb, gc], axis=1)
    wt = jnp.concatenate([qa, va], axis=1).T
    return w2.astype(BF16), wt.astype(BF16)


def _pack_w_qb(w):
    w3 = w.reshape(Q_LORA, H_B, NOPE_B + ROPE_B)
    w3 = jnp.pad(w3, ((0, 0), (0, 0), (0, HEAD_PAD - NOPE_B - ROPE_B)))
    return w3.reshape(Q_LORA, H_B * HEAD_PAD).T.astype(BF16)


def _pack_w_kvb(w):
    w3 = w.reshape(KV_LORA, H_B, NOPE_B + V_B)
    wkn = jnp.pad(w3[..., :NOPE_B], ((0, 0), (0, 0), (0, HEAD_PAD - NOPE_B)))
    wv = w3[..., NOPE_B:]
    return (wkn.reshape(KV_LORA, H_B * HEAD_PAD).astype(BF16),
            wv.reshape(KV_LORA, H_B * V_B).T.astype(BF16))


def _rope_tables(seq):
    t = jnp.arange(seq)
    row = (t // GRID_W).astype(F32)
    col = (t % GRID_W).astype(F32)

    def angles(rot_dim):
        axis_dim = rot_dim // 2
        inv = ROPE_BASE ** (-jnp.arange(0, axis_dim, 2, dtype=F32) / axis_dim)
        return jnp.concatenate([row[:, None] * inv, col[:, None] * inv], axis=-1)

    a = angles(HD_A)
    ca, sa = jnp.cos(a), jnp.sin(a)
    cos_a = jnp.tile(jnp.concatenate([ca, ca], axis=-1), (1, LANES // HD_A))
    sin_a = jnp.tile(jnp.concatenate([-sa, sa], axis=-1), (1, LANES // HD_A))
    b = angles(ROPE_B)
    cb, sb = jnp.cos(b), jnp.sin(b)
    tail = HEAD_PAD - NOPE_B - ROPE_B
    cos_b = jnp.concatenate([jnp.ones((seq, NOPE_B), F32), cb, cb, jnp.ones((seq, tail), F32)], axis=-1)
    sin_b = jnp.concatenate([jnp.zeros((seq, NOPE_B), F32), -sb, sb, jnp.zeros((seq, tail), F32)], axis=-1)
    return cos_a, sin_a, cos_b, sin_b, ca.T, sa.T, cb.T, sb.T


def kernel(x, c, ctx, c_ctx, w_mod, b_mod, g_pre, g_post, w_in, sink, g_qa, w_qb, g_kva, w_kvb,
           conv_w, w_branch, w_o):
    batch, seq, d = x.shape
    assert (seq, d) == (SEQ, D_MODEL) and ctx.shape == (batch, CTX_LEN, D_MODEL)
    assert batch <= 6
    depth = w_mod.shape[0]

    c_rows = jnp.concatenate([c, c_ctx[None, :], jnp.zeros((8 - batch - 1, d), F32)], axis=0)
    mod_all = _modulation(c_rows, w_mod, b_mod)
    tables = _rope_tables(seq)

    x2 = x.reshape(batch * seq, d)
    ctx2 = ctx.reshape(batch * CTX_LEN, d)
    x_tiles_front = seq // TM_FRONT
    x_tiles_merge = seq // TM_MERGE

    for i in range(depth):
        mod3 = mod_all[i].reshape(8, 1, 3 * d)
        w2, wt = _pack_w_in(w_in[i])
        wqbt = _pack_w_qb(w_qb[i])
        wkn, wvt = _pack_w_kvb(w_kvb[i])
        gpre = g_pre[i].reshape(1, d)
        gqa = g_qa[i].reshape(1, Q_LORA)
        gkva = g_kva[i].reshape(1, KV_LORA)
        gpost = g_post[i].reshape(1, d)
        wbr = w_branch[i].astype(BF16)
        wo = w_o[i].astype(BF16)

        fx = _front(x2, mod3, lambda t: t // x_tiles_front, gpre, w2, wt, gqa, wqbt, gkva, wkn,
                    wvt, tables, x_tiles_front)
        fc = _front(ctx2, mod3, lambda t: batch, gpre, w2, wt, gqa, wqbt, gkva, wkn, wvt, None, 1)
        qat, ka, vat, _, qbt, kb, vbt = fx[:7]
        qat_c, ka_c, vat_c, _, qbt_c, kb_c, vbt_c = fc[:7]

        ya = _window_attn(sink[i], qat, ka, vat, ka_c, vat_c, batch)
        yb = _mla_attn(qbt, kb, vbt, kb_c, vbt_c, batch)
        x2_new = _merge(ya, yb, fx, x2, mod3, lambda t: t // x_tiles_merge, gpost, conv_w[i],
                        wbr, wo, x_tiles_merge)
        if i < depth - 1:
            ya_c, yb_c = _ctx_attn(sink[i], qat_c, ka_c, vat_c, qbt_c, kb_c, vbt_c, batch)
            ctx2 = _merge(ya_c, yb_c, fc, ctx2, mod3, lambda t: batch, gpost, conv_w[i],
                          wbr, wo, CTX_LEN // TM_MERGE)
        x2 = x2_new
    return x2.reshape(batch, seq, d)
```

```python
import functools

import jax
import jax.numpy as jnp
import numpy as np
from jax import lax
from jax.experimental import pallas as pl
from jax.experimental.pallas import tpu as pltpu

F32 = jnp.float32
BF16 = jnp.bfloat16

D_MODEL = 1024
SEQ = 8192
DEPTH = 2
CTX_LEN = 256
GRID_W = 64
WINDOW = 128
ROPE_BASE = 10000.0
EPS = 1e-6
NEG_INF = -1e30
LOG2E = float(np.log2(np.e))

H_A = 8
HKV_A = 2
G_A = H_A // HKV_A
HD_A = 64
QA_SCALE = HD_A ** -0.5 * LOG2E

H_B = 8
Q_LORA = 384
KV_LORA = 256
NOPE_B = 64
ROPE_B = 32
V_B = 64
QB_SCALE = (NOPE_B + ROPE_B) ** -0.5 * LOG2E

W_C = 512
W_BRANCH = 512

IN_SIZES = (H_A * HD_A, HKV_A * HD_A, HKV_A * HD_A, W_BRANCH,
            Q_LORA, KV_LORA, ROPE_B, W_BRANCH,
            W_C, W_C, W_C, W_BRANCH,
            D_MODEL, D_MODEL, D_MODEL)

LANES = 128
HEAD_PAD = 128

C_KA = 0
C_ZA = 128
C_QL = 640
C_KVL = 1024
C_KR = 1280
C_ZB = 1408
C_BC = 1920
C_CC = 2432
C_UC = 2944
C_ZC = 3456
C_END = 3968
R_QA = 0
R_VA = H_A * HD_A
R_END = R_VA + HKV_A * HD_A

VMEM_LIMIT = 56 * 1024 * 1024

TM_FRONT = 512
TM_MERGE = 512
TQ_WIN = 256
TQ_MLA = 512
TK_MLA = 4096
KC_MLA = 512
KC_WIN = 256
SCORE_PAD = LANES


def _silu(v):
    return v * jax.nn.sigmoid(v)


def _rms(v, g):
    return v * lax.rsqrt(jnp.mean(v * v, axis=-1, keepdims=True) + EPS) * g


def _dot(a, b):
    return jnp.dot(a, b, preferred_element_type=F32)


def _dot_t(a, b):
    return lax.dot_general(a, b, (((1,), (1,)), ((), ())), preferred_element_type=F32)


def _mod_kernel(c_ref, w_ref, b_ref, o_ref):
    sc = _silu(c_ref[...])
    o_ref[0] = jnp.dot(sc, w_ref[0], preferred_element_type=F32,
                       precision=lax.Precision.HIGHEST) + b_ref[0]


def _modulation(c_rows, w_mod, b_mod):
    n_chunk = 3
    return pl.pallas_call(
        _mod_kernel,
        grid=(DEPTH, n_chunk),
        in_specs=[
            pl.BlockSpec((8, D_MODEL), lambda l, j: (0, 0)),
            pl.BlockSpec((1, D_MODEL, D_MODEL), lambda l, j: (l, 0, j)),
            pl.BlockSpec((1, 1, D_MODEL), lambda l, j: (l, 0, j)),
        ],
        out_specs=pl.BlockSpec((1, 8, D_MODEL), lambda l, j: (l, 0, j)),
        out_shape=jax.ShapeDtypeStruct((DEPTH, 8, 3 * D_MODEL), F32),
        compiler_params=pltpu.CompilerParams(
            dimension_semantics=("arbitrary", "arbitrary"), vmem_limit_bytes=VMEM_LIMIT),
        name="modulation",
    )(c_rows, w_mod, b_mod.reshape(DEPTH, 1, 3 * D_MODEL))


def _rope_lanes(v, cos, sin, first_half, up_shift, dn_shift):
    rot = jnp.where(first_half, pltpu.roll(v, up_shift, 1), pltpu.roll(v, dn_shift, 1))
    return v * cos + rot * sin


def _rope_rows(x1, x2, cos, sin):
    return x1 * cos - x2 * sin, x2 * cos + x1 * sin


def _front_kernel(x_ref, shift_ref, scale_ref, gpre_ref, w_ref, wt_ref, gqa_ref, wqbt_ref,
                  gkva_ref, wkn_ref, wvt_ref, *rest, rope):
    if rope:
        (cosa_ref, sina_ref, cosb_ref, sinb_ref,
         cosat_ref, sinat_ref, cosbt_ref, sinbt_ref) = rest[:8]
        outs = rest[8:]
    else:
        outs = rest
    (qat_ref, ka_ref, vat_ref, sza_ref, qbt_ref, kb_ref, vbt_ref, szb_ref,
     bc_ref, cu_ref, szc_ref) = outs

    x = x_ref[...]
    h = _rms(x, gpre_ref[...]) * (1.0 + scale_ref[0]) + shift_ref[0]
    h = h.astype(BF16)

    def mm(a, b):
        return _dot(h, w_ref[:, a:b])

    qv_t = _dot_t(wt_ref[...], h)
    half = HD_A // 2
    for hh in range(H_A):
        r0 = R_QA + hh * HD_A
        x1, x2 = qv_t[r0:r0 + half], qv_t[r0 + half:r0 + HD_A]
        if rope:
            x1, x2 = _rope_rows(x1, x2, cosat_ref[...], sinat_ref[...])
        qat_ref[hh * HD_A:hh * HD_A + half, :] = (x1 * QA_SCALE).astype(BF16)
        qat_ref[hh * HD_A + half:(hh + 1) * HD_A, :] = (x2 * QA_SCALE).astype(BF16)
    vat_ref[...] = qv_t[R_VA:R_END].astype(BF16)
    ka = mm(C_KA, C_ZA)
    if rope:
        lane = lax.broadcasted_iota(jnp.int32, (1, LANES), 1)
        ka = _rope_lanes(ka, cosa_ref[...], sina_ref[...], (lane % HD_A) < half,
                         LANES - half, half)
    ka_ref[...] = ka.astype(BF16)
    sza_ref[...] = _silu(mm(C_ZA, C_QL)).astype(BF16)

    qn = _rms(mm(C_QL, C_KVL), gqa_ref[...]).astype(BF16)
    qb_t = _dot_t(wqbt_ref[...], qn)
    hr = ROPE_B // 2
    for hh in range(H_B):
        r0 = hh * HEAD_PAD
        x1 = qb_t[r0 + NOPE_B:r0 + NOPE_B + hr]
        x2 = qb_t[r0 + NOPE_B + hr:r0 + NOPE_B + ROPE_B]
        if rope:
            x1, x2 = _rope_rows(x1, x2, cosbt_ref[...], sinbt_ref[...])
        qbt_ref[r0:r0 + NOPE_B, :] = (qb_t[r0:r0 + NOPE_B] * QB_SCALE).astype(BF16)
        qbt_ref[r0 + NOPE_B:r0 + NOPE_B + hr, :] = (x1 * QB_SCALE).astype(BF16)
        qbt_ref[r0 + NOPE_B + hr:r0 + NOPE_B + ROPE_B, :] = (x2 * QB_SCALE).astype(BF16)
        qbt_ref[r0 + NOPE_B + ROPE_B:r0 + HEAD_PAD, :] = jnp.zeros(
            (HEAD_PAD - NOPE_B - ROPE_B, x.shape[0]), BF16)
    kvn = _rms(mm(C_KVL, C_KR), gkva_ref[...]).astype(BF16)
    kn = _dot(kvn, wkn_ref[...])
    kr = mm(C_KR, C_ZB)
    if rope:
        kr = _rope_lanes(kr, cosb_ref[...], sinb_ref[...], lane < (NOPE_B + hr),
                         LANES - hr, hr)
    for hh in range(H_B):
        sl = slice(hh * HEAD_PAD, (hh + 1) * HEAD_PAD)
        kb_ref[:, sl] = (kn[:, sl] + kr).astype(BF16)
    vbt_ref[...] = _dot_t(wvt_ref[...], kvn).astype(BF16)
    szb_ref[...] = _silu(mm(C_ZB, C_BC)).astype(BF16)

    bc_ref[...] = mm(C_BC, C_CC).astype(BF16)
    cu_ref[...] = (mm(C_CC, C_UC) * mm(C_UC, C_ZC)).astype(BF16)
    szc_ref[...] = _silu(mm(C_ZC, C_END)).astype(BF16)


_FRONT_OUTS = ((True, H_A * HD_A), (False, HKV_A * HD_A), (True, HKV_A * HD_A), (False, W_BRANCH),
               (True, H_B * HEAD_PAD), (False, H_B * HEAD_PAD), (True, H_B * V_B), (False, W_BRANCH),
               (False, W_C), (False, W_C), (False, W_BRANCH))


def _front(rows, mod3, mod_row_fn, gpre, w2, wt, gqa, wqbt, gkva, wkn, wvt, tables, tiles_per_seq):
    t_rows = rows.shape[0]
    tm = TM_FRONT
    n_tiles = t_rows // tm
    rope = tables is not None
    const = lambda shape: pl.BlockSpec(shape, lambda i: (0,) * len(shape),
                                       pipeline_mode=pl.Buffered(1))
    in_specs = [
        pl.BlockSpec((tm, D_MODEL), lambda i: (i, 0)),
        pl.BlockSpec((1, 1, D_MODEL), lambda i: (mod_row_fn(i), 0, 0)),
        pl.BlockSpec((1, 1, D_MODEL), lambda i: (mod_row_fn(i), 0, 1)),
        const((1, D_MODEL)),
        const((D_MODEL, C_END)),
        const((R_END, D_MODEL)),
        const((1, Q_LORA)),
        const((H_B * HEAD_PAD, Q_LORA)),
        const((1, KV_LORA)),
        const((KV_LORA, H_B * HEAD_PAD)),
        const((H_B * V_B, KV_LORA)),
    ]
    args = [rows, mod3, mod3, gpre, w2, wt, gqa, wqbt, gkva, wkn, wvt]
    if rope:
        pos = lambda i: i % tiles_per_seq
        in_specs += [pl.BlockSpec((tm, LANES), lambda i: (pos(i), 0))] * 4
        in_specs += [pl.BlockSpec((HD_A // 2, tm), lambda i: (0, pos(i)))] * 2
        in_specs += [pl.BlockSpec((ROPE_B // 2, tm), lambda i: (0, pos(i)))] * 2
        args += list(tables)
    out_specs, out_shape = [], []
    for transposed, width in _FRONT_OUTS:
        if transposed:
            out_specs.append(pl.BlockSpec((width, tm), lambda i: (0, i)))
            out_shape.append(jax.ShapeDtypeStruct((width, t_rows), BF16))
        else:
            out_specs.append(pl.BlockSpec((tm, width), lambda i: (i, 0)))
            out_shape.append(jax.ShapeDtypeStruct((t_rows, width), BF16))
    return pl.pallas_call(
        functools.partial(_front_kernel, rope=rope),
        grid=(n_tiles,),
        in_specs=in_specs,
        out_specs=out_specs,
        out_shape=out_shape,
        compiler_params=pltpu.CompilerParams(
            dimension_semantics=("arbitrary",), vmem_limit_bytes=VMEM_LIMIT),
        name="front_rope" if rope else "front_ctx",
    )(*args)


def _score_scratch(nk, tq):
    return pltpu.VMEM((2, nk, tq + SCORE_PAD), F32)


def _ones_rows(kc):
    return (lax.broadcasted_iota(jnp.int32, (16, kc), 0) == 0).astype(BF16)


def _gqa_heads(sink_ref, qt_ref, k_chunks, vt_chunks, masks, s_scr, oT_scr):
    tq = qt_ref.shape[1]
    kc = k_chunks[0].shape[0]
    n_chunks = len(k_chunks)
    n_pairs = H_A // 2
    zeros = jnp.zeros((HD_A, tq), BF16)
    ones_rows = _ones_rows(kc)
    in_first = lax.broadcasted_iota(jnp.int32, (1, 2 * tq), 1) < tq

    def weights(j):
        cols = []
        for hh in (2 * j, 2 * j + 1):
            q_t = qt_ref[hh * HD_A:(hh + 1) * HD_A, :]
            cols.append(jnp.concatenate([q_t, zeros] if hh // G_A == 0 else [zeros, q_t], axis=0))
        return jnp.concatenate(cols, axis=1)

    def score_chunk(w, c, slot, mx):
        s = _dot(k_chunks[c], w)
        if masks[c] is not None:
            s = jnp.where(masks[c], s, NEG_INF)
        s_scr[slot, c * kc:(c + 1) * kc, 0:2 * tq] = s
        cm = jnp.max(s, axis=0, keepdims=True)
        return cm if mx is None else jnp.maximum(mx, cm)

    mx = None
    w = weights(0)
    for c in range(n_chunks):
        mx = score_chunk(w, c, 0, mx)
    for j in range(n_pairs):
        g = (2 * j) // G_A
        slot = j % 2
        sink = jnp.where(in_first, sink_ref[2 * j] * LOG2E, sink_ref[2 * j + 1] * LOG2E)
        m = jnp.maximum(mx, sink)
        pv = None
        mx = None
        if j + 1 < n_pairs:
            w = weights(j + 1)
        for c in range(n_chunks):
            if j + 1 < n_pairs:
                mx = score_chunk(w, c, 1 - slot, mx)
            p = jnp.exp2(s_scr[slot, c * kc:(c + 1) * kc, 0:2 * tq] - m).astype(BF16)
            vt = jnp.concatenate([vt_chunks[c][g * HD_A:(g + 1) * HD_A, :], ones_rows], axis=0)
            d = _dot(vt, p)
            pv = d if pv is None else pv + d
        o_t = pv[:HD_A] / (pv[HD_A:HD_A + 1] + jnp.exp2(sink - m))
        oT_scr[2 * j * HD_A:(2 * j + 1) * HD_A, :] = o_t[:, 0:tq]
        oT_scr[(2 * j + 1) * HD_A:(2 * j + 2) * HD_A, :] = o_t[:, tq:2 * tq]


def _mla_tile(qt_ref, k_fn, vt_fn, n_chunks, kc, m_scr, l_scr, acc_scr, s_scr):
    tq = qt_ref.shape[1]

    def score_chunk(hh, c, slot, mx):
        s = _dot(k_fn(hh, c), qt_ref[hh * HEAD_PAD:(hh + 1) * HEAD_PAD, :])
        s_scr[slot, c * kc:(c + 1) * kc, 0:tq] = s
        cm = jnp.max(s, axis=0, keepdims=True)
        return cm if mx is None else jnp.maximum(mx, cm)

    ones_rows = _ones_rows(kc)
    mx = None
    for c in range(n_chunks):
        mx = score_chunk(0, c, 0, mx)
    for hh in range(H_B):
        slot = hh % 2
        m_prev = m_scr[hh]
        m_new = jnp.maximum(m_prev, mx)
        alpha = jnp.exp2(m_prev - m_new)
        m_scr[hh] = m_new
        pv = None
        mx = None
        for c in range(n_chunks):
            if hh + 1 < H_B:
                mx = score_chunk(hh + 1, c, 1 - slot, mx)
            p = jnp.exp2(s_scr[slot, c * kc:(c + 1) * kc, 0:tq] - m_new).astype(BF16)
            d = _dot(jnp.concatenate([vt_fn(hh, c), ones_rows], axis=0), p)
            pv = d if pv is None else pv + d
        l_scr[hh] = alpha * l_scr[hh] + pv[V_B:V_B + 1]
        rows = slice(hh * V_B, (hh + 1) * V_B)
        acc_scr[rows, :] = alpha * acc_scr[rows, :] + pv[:V_B]


def _mla_init(m_scr, l_scr, acc_scr):
    m_scr[...] = jnp.full(m_scr.shape, NEG_INF, F32)
    l_scr[...] = jnp.zeros(l_scr.shape, F32)
    acc_scr[...] = jnp.zeros(acc_scr.shape, F32)


def _mla_finish(o_ref, l_scr, acc_scr):
    for hh in range(H_B):
        rows = slice(hh * V_B, (hh + 1) * V_B)
        acc_scr[rows, :] = acc_scr[rows, :] / l_scr[hh]
    o_ref[...] = acc_scr[...].T.astype(o_ref.dtype)


def _win_kernel(sink_ref, qt_ref, kp_ref, kc_ref, kn_ref, kx_ref, vp_ref, vc_ref, vn_ref, vx_ref,
                o_ref, oT_scr, s_scr):
    n = pl.program_id(1)
    tq = qt_ref.shape[1]
    kc = KC_WIN
    n_loc = (tq + 2 * WINDOW) // kc
    k_loc = jnp.concatenate([kp_ref[...], kc_ref[...], kn_ref[...]], axis=0)
    vt_loc = jnp.concatenate([vp_ref[...], vc_ref[...], vn_ref[...]], axis=1)
    k_chunks = [k_loc[c * kc:(c + 1) * kc, :] for c in range(n_loc)] + [kx_ref[...]]
    vt_chunks = [vt_loc[:, c * kc:(c + 1) * kc] for c in range(n_loc)] + [vx_ref[...]]
    col = lax.broadcasted_iota(jnp.int32, (kc, 2 * tq), 1)
    col = jnp.where(col >= tq, col - tq, col)
    masks = []
    for c in range(n_loc):
        r = lax.broadcasted_iota(jnp.int32, (kc, 2 * tq), 0) + c * kc
        kabs = n * tq - WINDOW + r
        masks.append((jnp.abs(r - WINDOW - col) <= WINDOW) & (kabs >= 0) & (kabs < SEQ))
    masks.append(None)
    _gqa_heads(sink_ref, qt_ref, k_chunks, vt_chunks, masks, s_scr, oT_scr)
    o_ref[...] = oT_scr[...].T.astype(o_ref.dtype)


def _window_attn(sink, qat, ka, vat, ka_c, vat_c, batch):
    tq = TQ_WIN
    nq = SEQ // tq
    sub = tq // WINDOW
    nw = SEQ // WINDOW
    kv_w = HKV_A * HD_A
    prev_i = lambda b, n: b * nw + jnp.maximum(n * sub - 1, 0)
    next_i = lambda b, n: b * nw + jnp.minimum((n + 1) * sub, nw - 1)
    grid_spec = pltpu.PrefetchScalarGridSpec(
        num_scalar_prefetch=1,
        grid=(batch, nq),
        in_specs=[
            pl.BlockSpec((H_A * HD_A, tq), lambda b, n, s: (0, b * nq + n)),
            pl.BlockSpec((WINDOW, kv_w), lambda b, n, s: (prev_i(b, n), 0)),
            pl.BlockSpec((tq, kv_w), lambda b, n, s: (b * nq + n, 0)),
            pl.BlockSpec((WINDOW, kv_w), lambda b, n, s: (next_i(b, n), 0)),
            pl.BlockSpec((CTX_LEN, kv_w), lambda b, n, s: (b, 0)),
            pl.BlockSpec((kv_w, WINDOW), lambda b, n, s: (0, prev_i(b, n))),
            pl.BlockSpec((kv_w, tq), lambda b, n, s: (0, b * nq + n)),
            pl.BlockSpec((kv_w, WINDOW), lambda b, n, s: (0, next_i(b, n))),
            pl.BlockSpec((kv_w, CTX_LEN), lambda b, n, s: (0, b)),
        ],
        out_specs=pl.BlockSpec((tq, H_A * HD_A), lambda b, n, s: (b * nq + n, 0)),
        scratch_shapes=[pltpu.VMEM((H_A * HD_A, tq), F32), _score_scratch(tq + 2 * WINDOW + CTX_LEN, 2 * tq)],
    )
    return pl.pallas_call(
        _win_kernel,
        grid_spec=grid_spec,
        out_shape=jax.ShapeDtypeStruct((batch * SEQ, H_A * HD_A), BF16),
        compiler_params=pltpu.CompilerParams(
            dimension_semantics=("arbitrary", "arbitrary"), vmem_limit_bytes=VMEM_LIMIT),
        name="window_attn",
    )(sink, qat, ka, ka, ka, ka_c, vat, vat, vat, vat_c)


def _mla_ctx_tile(qt_ref, kc_ref, vct_ref, m_scr, l_scr, acc_scr, s_scr):
    _mla_tile(qt_ref,
              lambda hh, c: kc_ref[:, hh * HEAD_PAD:(hh + 1) * HEAD_PAD],
              lambda hh, c: vct_ref[hh * V_B:(hh + 1) * V_B, :],
              1, kc_ref.shape[0], m_scr, l_scr, acc_scr, s_scr)


def _mla_kernel(qt_ref, kx_ref, vxt_ref, kc_ref, vct_ref, o_ref, m_scr, l_scr, acc_scr, s_scr):
    _mla_init(m_scr, l_scr, acc_scr)
    n_tiles = kx_ref.shape[0] // TK_MLA
    kc = KC_MLA

    def body(j, carry):
        def rows(c):
            return pl.ds(pl.multiple_of(j * TK_MLA + c * kc, kc), kc)

        _mla_tile(qt_ref,
                  lambda hh, c: kx_ref[rows(c), hh * HEAD_PAD:(hh + 1) * HEAD_PAD],
                  lambda hh, c: vxt_ref[hh * V_B:(hh + 1) * V_B, rows(c)],
                  TK_MLA // kc, kc, m_scr, l_scr, acc_scr, s_scr)
        return carry

    lax.fori_loop(0, n_tiles, body, 0)
    _mla_ctx_tile(qt_ref, kc_ref, vct_ref, m_scr, l_scr, acc_scr, s_scr)
    _mla_finish(o_ref, l_scr, acc_scr)


def _mla_scratch(tq, tk):
    return [pltpu.VMEM((H_B, 1, tq), F32), pltpu.VMEM((H_B, 1, tq), F32),
            pltpu.VMEM((H_B * V_B, tq), F32), _score_scratch(tk, tq)]


def _mla_attn(qbt, kb, vbt, kb_c, vbt_c, batch):
    tq = TQ_MLA
    nq = SEQ // tq
    res = lambda shape, fn: pl.BlockSpec(shape, fn, pipeline_mode=pl.Buffered(1))
    return pl.pallas_call(
        _mla_kernel,
        grid=(batch, nq),
        in_specs=[
            pl.BlockSpec((H_B * HEAD_PAD, tq), lambda b, i: (0, b * nq + i)),
            res((SEQ, H_B * HEAD_PAD), lambda b, i: (b, 0)),
            res((H_B * V_B, SEQ), lambda b, i: (0, b)),
            res((CTX_LEN, H_B * HEAD_PAD), lambda b, i: (b, 0)),
            res((H_B * V_B, CTX_LEN), lambda b, i: (0, b)),
        ],
        out_specs=pl.BlockSpec((tq, H_B * V_B), lambda b, i: (b * nq + i, 0)),
        out_shape=jax.ShapeDtypeStruct((batch * SEQ, H_B * V_B), BF16),
        scratch_shapes=_mla_scratch(tq, TK_MLA),
        compiler_params=pltpu.CompilerParams(
            dimension_semantics=("arbitrary", "arbitrary"), vmem_limit_bytes=VMEM_LIMIT),
        name="mla_attn",
    )(qbt, kb, vbt, kb_c, vbt_c)


def _ctx_attn_kernel(sink_ref, qat_ref, ka_ref, vat_ref, qbt_ref, kb_ref, vbt_ref,
                     ya_ref, yb_ref, oT_scr, m_scr, l_scr, acc_scr, s_scr):
    _gqa_heads(sink_ref, qat_ref, [ka_ref[...]], [vat_ref[...]], [None], s_scr, oT_scr)
    ya_ref[...] = oT_scr[...].T.astype(ya_ref.dtype)
    _mla_init(m_scr, l_scr, acc_scr)
    _mla_ctx_tile(qbt_ref, kb_ref, vbt_ref, m_scr, l_scr, acc_scr, s_scr)
    _mla_finish(yb_ref, l_scr, acc_scr)


def _ctx_attn(sink, qat_c, ka_c, vat_c, qbt_c, kb_c, vbt_c, batch):
    tok = lambda cols: pl.BlockSpec((CTX_LEN, cols), lambda b, s: (b, 0))
    feat = lambda rows: pl.BlockSpec((rows, CTX_LEN), lambda b, s: (0, b))
    grid_spec = pltpu.PrefetchScalarGridSpec(
        num_scalar_prefetch=1,
        grid=(batch,),
        in_specs=[feat(H_A * HD_A), tok(HKV_A * HD_A), feat(HKV_A * HD_A),
                  feat(H_B * HEAD_PAD), tok(H_B * HEAD_PAD), feat(H_B * V_B)],
        out_specs=[tok(H_A * HD_A), tok(H_B * V_B)],
        scratch_shapes=([pltpu.VMEM((H_A * HD_A, CTX_LEN), F32)] + _mla_scratch(CTX_LEN, CTX_LEN)[:3]
                        + [_score_scratch(CTX_LEN, 2 * CTX_LEN)]),
    )
    return pl.pallas_call(
        _ctx_attn_kernel,
        grid_spec=grid_spec,
        out_shape=[jax.ShapeDtypeStruct((batch * CTX_LEN, W_BRANCH), BF16)] * 2,
        compiler_params=pltpu.CompilerParams(
            dimension_semantics=("arbitrary",), vmem_limit_bytes=VMEM_LIMIT),
        name="ctx_attn",
    )(sink, qat_c, ka_c, vat_c, qbt_c, kb_c, vbt_c)


def _merge_kernel(ya_ref, yb_ref, sza_ref, szb_ref, szc_ref, bc_ref, cu_ref, cup_ref, cun_ref,
                  x_ref, shift_ref, scale_ref, gate_ref, gpre_ref, gpost_ref, cw_ref, wg_ref,
                  wbr_ref, wo_ref, o_ref, *, tiles_per_seq):
    i = pl.program_id(0)
    tm = x_ref.shape[0]
    x = x_ref[...]
    h = (_rms(x, gpre_ref[...]) * (1.0 + scale_ref[0]) + shift_ref[0]).astype(BF16)
    pos = i % tiles_per_seq
    z = cu_ref[...].astype(F32)
    prev_row = jnp.where(pos == 0, 0.0, cup_ref[7:8, :].astype(F32))
    next_row = jnp.where(pos == tiles_per_seq - 1, 0.0, cun_ref[0:1, :].astype(F32))
    row = lax.broadcasted_iota(jnp.int32, z.shape, 0)
    z_m1 = jnp.where(row == 0, prev_row, pltpu.roll(z, 1, 0))
    z_p1 = jnp.where(row == tm - 1, next_row, pltpu.roll(z, tm - 1, 0))
    conv = cw_ref[0:1, :] * z_m1 + cw_ref[1:2, :] * z + cw_ref[2:3, :] * z_p1
    yc = bc_ref[...].astype(F32) * conv

    def branch(y, sz_ref, k):
        t = (y * sz_ref[...].astype(F32)).astype(BF16)
        sg = jax.nn.sigmoid(_dot(h, wg_ref[:, k * D_MODEL:(k + 1) * D_MODEL]))
        return sg * _dot(t, wbr_ref[k])

    m = (branch(ya_ref[...].astype(F32), sza_ref, 0)
         + branch(yb_ref[...].astype(F32), szb_ref, 1)
         + branch(yc, szc_ref, 2))
    out = _dot(m.astype(BF16), wo_ref[...])
    o_ref[...] = x + gate_ref[0] * _rms(out, gpost_ref[...])


def _merge(ya, yb, fr, resid, mod3, mod_row_of_seq, gpre, gpost, conv_w, wg, wbr, wo,
           tiles_per_seq_rows):
    (_, _, _, sza, _, _, _, szb, bc, cu, szc) = fr
    t_rows = resid.shape[0]
    tm = min(TM_MERGE, tiles_per_seq_rows)
    tiles_per_seq = tiles_per_seq_rows // tm
    n_tiles = t_rows // tm
    sub = tm // 8
    n_sub = t_rows // 8
    mod_row_fn = lambda i: mod_row_of_seq(i // tiles_per_seq)
    row = lambda cols: pl.BlockSpec((tm, cols), lambda i: (i, 0))
    const = lambda shape: pl.BlockSpec(shape, lambda i: (0,) * len(shape),
                                       pipeline_mode=pl.Buffered(1))
    in_specs = [
        row(512), row(512), row(512), row(512), row(512), row(512), row(512),
        pl.BlockSpec((8, 512), lambda i: (jnp.maximum(i * sub - 1, 0), 0)),
        pl.BlockSpec((8, 512), lambda i: (jnp.minimum((i + 1) * sub, n_sub - 1), 0)),
        row(1024),
        pl.BlockSpec((1, 1, D_MODEL), lambda i: (mod_row_fn(i), 0, 0)),
        pl.BlockSpec((1, 1, D_MODEL), lambda i: (mod_row_fn(i), 0, 1)),
        pl.BlockSpec((1, 1, D_MODEL), lambda i: (mod_row_fn(i), 0, 2)),
        const((1, D_MODEL)),
        const((1, D_MODEL)),
        const((3, W_C)),
        const((D_MODEL, 3 * D_MODEL)),
        const((3, W_BRANCH, D_MODEL)),
        const((D_MODEL, D_MODEL)),
    ]
    return pl.pallas_call(
        functools.partial(_merge_kernel, tiles_per_seq=tiles_per_seq),
        grid=(n_tiles,),
        in_specs=in_specs,
        out_specs=pl.BlockSpec((tm, D_MODEL), lambda i: (i, 0)),
        out_shape=jax.ShapeDtypeStruct((t_rows, D_MODEL), F32),
        compiler_params=pltpu.CompilerParams(
            dimension_semantics=("arbitrary",), vmem_limit_bytes=VMEM_LIMIT),
        name="merge",
    )(ya, yb, sza, szb, szc, bc, cu, cu, cu, resid, mod3, mod3, mod3, gpre, gpost, conv_w, wg,
      wbr, wo)


def _pack_w_in(w):
    offs = np.concatenate([[0], np.cumsum(IN_SIZES)])
    seg = [w[:, offs[k]:offs[k + 1]] for k in range(len(IN_SIZES))]
    qa, ka, va, za, ql, kvl, kr, zb, bc, cc, uc, zc, ga, gb, gc = seg
    d = w.shape[0]
    krp = jnp.concatenate([jnp.zeros((d, NOPE_B), w.dtype), kr,
                           jnp.zeros((d, HEAD_PAD - NOPE_B - ROPE_B), w.dtype)], axis=1)
    w2 = jnp.concatenate([ka, za, ql, kvl, krp, zb, bc, cc, uc, zc], axis=1)
    wt = jnp.concatenate([qa, va], axis=1).T
    wg = jnp.concatenate([ga, gb, gc], axis=1)
    return w2.astype(BF16), wt.astype(BF16), wg.astype(BF16)


def _pack_w_qb(w):
    w3 = w.reshape(Q_LORA, H_B, NOPE_B + ROPE_B)
    w3 = jnp.pad(w3, ((0, 0), (0, 0), (0, HEAD_PAD - NOPE_B - ROPE_B)))
    return w3.reshape(Q_LORA, H_B * HEAD_PAD).T.astype(BF16)


def _pack_w_kvb(w):
    w3 = w.reshape(KV_LORA, H_B, NOPE_B + V_B)
    wkn = jnp.pad(w3[..., :NOPE_B], ((0, 0), (0, 0), (0, HEAD_PAD - NOPE_B)))
    wv = w3[..., NOPE_B:]
    return (wkn.reshape(KV_LORA, H_B * HEAD_PAD).astype(BF16),
            wv.reshape(KV_LORA, H_B * V_B).T.astype(BF16))


def _rope_tables(seq):
    t = jnp.arange(seq)
    row = (t // GRID_W).astype(F32)
    col = (t % GRID_W).astype(F32)

    def angles(rot_dim):
        axis_dim = rot_dim // 2
        inv = ROPE_BASE ** (-jnp.arange(0, axis_dim, 2, dtype=F32) / axis_dim)
        return jnp.concatenate([row[:, None] * inv, col[:, None] * inv], axis=-1)

    a = angles(HD_A)
    ca, sa = jnp.cos(a), jnp.sin(a)
    cos_a = jnp.tile(jnp.concatenate([ca, ca], axis=-1), (1, LANES // HD_A))
    sin_a = jnp.tile(jnp.concatenate([-sa, sa], axis=-1), (1, LANES // HD_A))
    b = angles(ROPE_B)
    cb, sb = jnp.cos(b), jnp.sin(b)
    tail = HEAD_PAD - NOPE_B - ROPE_B
    cos_b = jnp.concatenate([jnp.ones((seq, NOPE_B), F32), cb, cb, jnp.ones((seq, tail), F32)], axis=-1)
    sin_b = jnp.concatenate([jnp.zeros((seq, NOPE_B), F32), -sb, sb, jnp.zeros((seq, tail), F32)], axis=-1)
    return cos_a, sin_a, cos_b, sin_b, ca.T, sa.T, cb.T, sb.T


def kernel(x, c, ctx, c_ctx, w_mod, b_mod, g_pre, g_post, w_in, sink, g_qa, w_qb, g_kva, w_kvb,
           conv_w, w_branch, w_o):
    batch, seq, d = x.shape
    assert (seq, d) == (SEQ, D_MODEL) and ctx.shape == (batch, CTX_LEN, D_MODEL)
    assert batch <= 6
    depth = w_mod.shape[0]

    c_rows = jnp.concatenate([c, c_ctx[None, :], jnp.zeros((8 - batch - 1, d), F32)], axis=0)
    mod_all = _modulation(c_rows, w_mod, b_mod)
    tables = _rope_tables(seq)

    x2 = x.reshape(batch * seq, d)
    ctx2 = ctx.reshape(batch * CTX_LEN, d)
    x_tiles_front = seq // TM_FRONT

    for i in range(depth):
        mod3 = mod_all[i].reshape(8, 1, 3 * d)
        w2, wt, wg = _pack_w_in(w_in[i])
        wqbt = _pack_w_qb(w_qb[i])
        wkn, wvt = _pack_w_kvb(w_kvb[i])
        gpre = g_pre[i].reshape(1, d)
        gqa = g_qa[i].reshape(1, Q_LORA)
        gkva = g_kva[i].reshape(1, KV_LORA)
        gpost = g_post[i].reshape(1, d)
        wbr = w_branch[i].astype(BF16)
        wo = w_o[i].astype(BF16)

        fx = _front(x2, mod3, lambda t: t // x_tiles_front, gpre, w2, wt, gqa, wqbt, gkva, wkn,
                    wvt, tables, x_tiles_front)
        fc = _front(ctx2, mod3, lambda t: batch, gpre, w2, wt, gqa, wqbt, gkva, wkn, wvt, None, 1)
        qat, ka, vat, _, qbt, kb, vbt = fx[:7]
        qat_c, ka_c, vat_c, _, qbt_c, kb_c, vbt_c = fc[:7]

        ya = _window_attn(sink[i], qat, ka, vat, ka_c, vat_c, batch)
        yb = _mla_attn(qbt, kb, vbt, kb_c, vbt_c, batch)
        x2_new = _merge(ya, yb, fx, x2, mod3, lambda b: b, gpre, gpost,
                        conv_w[i], wg, wbr, wo, seq)
        if i < depth - 1:
            ya_c, yb_c = _ctx_attn(sink[i], qat_c, ka_c, vat_c, qbt_c, kb_c, vbt_c, batch)
            ctx2 = _merge(ya_c, yb_c, fc, ctx2, mod3, lambda b: batch, gpre, gpost,
                          conv_w[i], wg, wbr, wo, CTX_LEN)
        x2 = x2_new
    return x2.reshape(batch, seq, d)
```

```python
import functools

import jax
import jax.numpy as jnp
import numpy as np
from jax import lax
from jax.experimental import pallas as pl
from jax.experimental.pallas import tpu as pltpu

F32 = jnp.float32
BF16 = jnp.bfloat16

D_MODEL = 1024
SEQ = 8192
DEPTH = 2
CTX_LEN = 256
GRID_W = 64
WINDOW = 128
ROPE_BASE = 10000.0
EPS = 1e-6
NEG_INF = -1e30
LOG2E = float(np.log2(np.e))

H_A = 8
HKV_A = 2
G_A = H_A // HKV_A
HD_A = 64
QA_SCALE = HD_A ** -0.5 * LOG2E

H_B = 8
Q_LORA = 384
KV_LORA = 256
NOPE_B = 64
ROPE_B = 32
V_B = 64
QB_SCALE = (NOPE_B + ROPE_B) ** -0.5 * LOG2E

W_C = 512
W_BRANCH = 512

IN_SIZES = (H_A * HD_A, HKV_A * HD_A, HKV_A * HD_A, W_BRANCH,
            Q_LORA, KV_LORA, ROPE_B, W_BRANCH,
            W_C, W_C, W_C, W_BRANCH,
            D_MODEL, D_MODEL, D_MODEL)

LANES = 128
HEAD_PAD = 128

C_KA = 0
C_ZA = 128
C_QL = 640
C_KVL = 1024
C_KR = 1280
C_ZB = 1408
C_BC = 1920
C_CC = 2432
C_UC = 2944
C_ZC = 3456
C_END = 3968
R_QA = 0
R_VA = H_A * HD_A
R_END = R_VA + HKV_A * HD_A

VMEM_LIMIT = 56 * 1024 * 1024

TM_FRONT = 512
TM_MERGE = 512
TQ_WIN = 256
TQ_MLA = 512
TK_MLA = 4096
KC_MLA = 512
KC_WIN = 256
SCORE_PAD = LANES


def _silu(v):
    return v * jax.nn.sigmoid(v)


def _rms(v, g):
    return v * lax.rsqrt(jnp.mean(v * v, axis=-1, keepdims=True) + EPS) * g


def _dot(a, b):
    return jnp.dot(a, b, preferred_element_type=F32)


def _dot_t(a, b):
    return lax.dot_general(a, b, (((1,), (1,)), ((), ())), preferred_element_type=F32)


def _mod_kernel(c_ref, w_ref, b_ref, o_ref):
    sc = _silu(c_ref[...])
    o_ref[0] = jnp.dot(sc, w_ref[0], preferred_element_type=F32,
                       precision=lax.Precision.HIGHEST) + b_ref[0]


def _modulation(c_rows, w_mod, b_mod):
    n_chunk = 3
    return pl.pallas_call(
        _mod_kernel,
        grid=(DEPTH, n_chunk),
        in_specs=[
            pl.BlockSpec((8, D_MODEL), lambda l, j: (0, 0)),
            pl.BlockSpec((1, D_MODEL, D_MODEL), lambda l, j: (l, 0, j)),
            pl.BlockSpec((1, 1, D_MODEL), lambda l, j: (l, 0, j)),
        ],
        out_specs=pl.BlockSpec((1, 8, D_MODEL), lambda l, j: (l, 0, j)),
        out_shape=jax.ShapeDtypeStruct((DEPTH, 8, 3 * D_MODEL), F32),
        compiler_params=pltpu.CompilerParams(
            dimension_semantics=("arbitrary", "arbitrary"), vmem_limit_bytes=VMEM_LIMIT),
        name="modulation",
    )(c_rows, w_mod, b_mod.reshape(DEPTH, 1, 3 * D_MODEL))


def _rope_lanes(v, cos, sin, first_half, up_shift, dn_shift):
    rot = jnp.where(first_half, pltpu.roll(v, up_shift, 1), pltpu.roll(v, dn_shift, 1))
    return v * cos + rot * sin


def _rope_rows(x1, x2, cos, sin):
    return x1 * cos - x2 * sin, x2 * cos + x1 * sin


def _front_kernel(x_ref, shift_ref, scale_ref, gpre_ref, w_ref, wt_ref, gqa_ref, wqbt_ref,
                  gkva_ref, wkn_ref, wvt_ref, *rest, rope):
    if rope:
        (cosa_ref, sina_ref, cosb_ref, sinb_ref,
         cosat_ref, sinat_ref, cosbt_ref, sinbt_ref) = rest[:8]
        outs = rest[8:]
    else:
        outs = rest
    (qat_ref, ka_ref, vat_ref, sza_ref, qbt_ref, kb_ref, vbt_ref, szb_ref,
     bc_ref, cu_ref, szc_ref) = outs

    x = x_ref[...]
    h = _rms(x, gpre_ref[...]) * (1.0 + scale_ref[0]) + shift_ref[0]
    h = h.astype(BF16)

    def mm(a, b):
        return _dot(h, w_ref[:, a:b])

    qv_t = _dot_t(wt_ref[...], h)
    half = HD_A // 2
    for hh in range(H_A):
        r0 = R_QA + hh * HD_A
        x1, x2 = qv_t[r0:r0 + half], qv_t[r0 + half:r0 + HD_A]
        if rope:
            x1, x2 = _rope_rows(x1, x2, cosat_ref[...], sinat_ref[...])
        qat_ref[hh * HD_A:hh * HD_A + half, :] = (x1 * QA_SCALE).astype(BF16)
        qat_ref[hh * HD_A + half:(hh + 1) * HD_A, :] = (x2 * QA_SCALE).astype(BF16)
    vat_ref[...] = qv_t[R_VA:R_END].astype(BF16)
    ka = mm(C_KA, C_ZA)
    if rope:
        lane = lax.broadcasted_iota(jnp.int32, (1, LANES), 1)
        ka = _rope_lanes(ka, cosa_ref[...], sina_ref[...], (lane % HD_A) < half,
                         LANES - half, half)
    ka_ref[...] = ka.astype(BF16)
    sza_ref[...] = _silu(mm(C_ZA, C_QL)).astype(BF16)

    qn = _rms(mm(C_QL, C_KVL), gqa_ref[...]).astype(BF16)
    qb_t = _dot_t(wqbt_ref[...], qn)
    hr = ROPE_B // 2
    for hh in range(H_B):
        r0 = hh * HEAD_PAD
        x1 = qb_t[r0 + NOPE_B:r0 + NOPE_B + hr]
        x2 = qb_t[r0 + NOPE_B + hr:r0 + NOPE_B + ROPE_B]
        if rope:
            x1, x2 = _rope_rows(x1, x2, cosbt_ref[...], sinbt_ref[...])
        qbt_ref[r0:r0 + NOPE_B, :] = (qb_t[r0:r0 + NOPE_B] * QB_SCALE).astype(BF16)
        qbt_ref[r0 + NOPE_B:r0 + NOPE_B + hr, :] = (x1 * QB_SCALE).astype(BF16)
        qbt_ref[r0 + NOPE_B + hr:r0 + NOPE_B + ROPE_B, :] = (x2 * QB_SCALE).astype(BF16)
        qbt_ref[r0 + NOPE_B + ROPE_B:r0 + HEAD_PAD, :] = jnp.zeros(
            (HEAD_PAD - NOPE_B - ROPE_B, x.shape[0]), BF16)
    kvn = _rms(mm(C_KVL, C_KR), gkva_ref[...]).astype(BF16)
    kn = _dot(kvn, wkn_ref[...])
    kr = mm(C_KR, C_ZB)
    if rope:
        kr = _rope_lanes(kr, cosb_ref[...], sinb_ref[...], lane < (NOPE_B + hr),
                         LANES - hr, hr)
    for hh in range(H_B):
        sl = slice(hh * HEAD_PAD, (hh + 1) * HEAD_PAD)
        kb_ref[:, sl] = (kn[:, sl] + kr).astype(BF16)
    vbt_ref[...] = _dot_t(wvt_ref[...], kvn).astype(BF16)
    szb_ref[...] = _silu(mm(C_ZB, C_BC)).astype(BF16)

    bc_ref[...] = mm(C_BC, C_CC).astype(BF16)
    cu_ref[...] = (mm(C_CC, C_UC) * mm(C_UC, C_ZC)).astype(BF16)
    szc_ref[...] = _silu(mm(C_ZC, C_END)).astype(BF16)


_FRONT_OUTS = ((True, H_A * HD_A), (False, HKV_A * HD_A), (True, HKV_A * HD_A), (False, W_BRANCH),
               (True, H_B * HEAD_PAD), (False, H_B * HEAD_PAD), (True, H_B * V_B), (False, W_BRANCH),
               (False, W_C), (False, W_C), (False, W_BRANCH))


def _front(rows, mod3, mod_row_fn, gpre, w2, wt, gqa, wqbt, gkva, wkn, wvt, tables, tiles_per_seq):
    t_rows = rows.shape[0]
    tm = TM_FRONT
    n_tiles = t_rows // tm
    rope = tables is not None
    const = lambda shape: pl.BlockSpec(shape, lambda i: (0,) * len(shape),
                                       pipeline_mode=pl.Buffered(1))
    in_specs = [
        pl.BlockSpec((tm, D_MODEL), lambda i: (i, 0)),
        pl.BlockSpec((1, 1, D_MODEL), lambda i: (mod_row_fn(i), 0, 0)),
        pl.BlockSpec((1, 1, D_MODEL), lambda i: (mod_row_fn(i), 0, 1)),
        const((1, D_MODEL)),
        const((D_MODEL, C_END)),
        const((R_END, D_MODEL)),
        const((1, Q_LORA)),
        const((H_B * HEAD_PAD, Q_LORA)),
        const((1, KV_LORA)),
        const((KV_LORA, H_B * HEAD_PAD)),
        const((H_B * V_B, KV_LORA)),
    ]
    args = [rows, mod3, mod3, gpre, w2, wt, gqa, wqbt, gkva, wkn, wvt]
    if rope:
        pos = lambda i: i % tiles_per_seq
        in_specs += [pl.BlockSpec((tm, LANES), lambda i: (pos(i), 0))] * 4
        in_specs += [pl.BlockSpec((HD_A // 2, tm), lambda i: (0, pos(i)))] * 2
        in_specs += [pl.BlockSpec((ROPE_B // 2, tm), lambda i: (0, pos(i)))] * 2
        args += list(tables)
    out_specs, out_shape = [], []
    for transposed, width in _FRONT_OUTS:
        if transposed:
            out_specs.append(pl.BlockSpec((width, tm), lambda i: (0, i)))
            out_shape.append(jax.ShapeDtypeStruct((width, t_rows), BF16))
        else:
            out_specs.append(pl.BlockSpec((tm, width), lambda i: (i, 0)))
            out_shape.append(jax.ShapeDtypeStruct((t_rows, width), BF16))
    return pl.pallas_call(
        functools.partial(_front_kernel, rope=rope),
        grid=(n_tiles,),
        in_specs=in_specs,
        out_specs=out_specs,
        out_shape=out_shape,
        compiler_params=pltpu.CompilerParams(
            dimension_semantics=("arbitrary",), vmem_limit_bytes=VMEM_LIMIT),
        name="front_rope" if rope else "front_ctx",
    )(*args)


def _score_scratch(nk, tq):
    return pltpu.VMEM((2, nk, tq + SCORE_PAD), F32)


def _ones_rows(kc):
    return (lax.broadcasted_iota(jnp.int32, (16, kc), 0) == 0).astype(BF16)


def _gqa_heads(sink_ref, qt_ref, chunks, bias_scr, s_scr, oT_scr):
    tq = qt_ref.shape[1]
    sizes = [k_ref.shape[0] for k_ref, _, _ in chunks]
    offs = [sum(sizes[:c]) for c in range(len(chunks))]
    n_chunks = len(chunks)
    n_pairs = H_A // 2
    zeros = jnp.zeros((HD_A, tq), BF16)
    ones_rows = {kc: _ones_rows(kc) for kc in set(sizes)}
    in_first = lax.broadcasted_iota(jnp.int32, (1, 2 * tq), 1) < tq

    def weights(j):
        cols = []
        for hh in (2 * j, 2 * j + 1):
            q_t = qt_ref[hh * HD_A:(hh + 1) * HD_A, :]
            cols.append(jnp.concatenate([q_t, zeros] if hh // G_A == 0 else [zeros, q_t], axis=0))
        return jnp.concatenate(cols, axis=1)

    def score_chunk(w, c, slot, mx):
        k_ref, _, masked = chunks[c]
        rows = slice(offs[c], offs[c] + sizes[c])
        s = _dot(k_ref[...], w)
        if masked:
            s = s + bias_scr[rows, :]
        s_scr[slot, rows, 0:2 * tq] = s
        cm = jnp.max(s, axis=0, keepdims=True)
        return cm if mx is None else jnp.maximum(mx, cm)

    mx = None
    w = weights(0)
    for c in range(n_chunks):
        mx = score_chunk(w, c, 0, mx)
    for j in range(n_pairs):
        g = (2 * j) // G_A
        slot = j % 2
        sink = jnp.where(in_first, sink_ref[2 * j] * LOG2E, sink_ref[2 * j + 1] * LOG2E)
        m = jnp.maximum(mx, sink)
        pv = None
        mx = None
        if j + 1 < n_pairs:
            w = weights(j + 1)
        for c in range(n_chunks):
            if j + 1 < n_pairs:
                mx = score_chunk(w, c, 1 - slot, mx)
            rows = slice(offs[c], offs[c] + sizes[c])
            p = jnp.exp2(s_scr[slot, rows, 0:2 * tq] - m).astype(BF16)
            vt = jnp.concatenate([chunks[c][1][g * HD_A:(g + 1) * HD_A, :], ones_rows[sizes[c]]],
                                 axis=0)
            d = _dot(vt, p)
            pv = d if pv is None else pv + d
        o_t = pv[:HD_A] / (pv[HD_A:HD_A + 1] + jnp.exp2(sink - m))
        oT_scr[2 * j * HD_A:(2 * j + 1) * HD_A, :] = o_t[:, 0:tq]
        oT_scr[(2 * j + 1) * HD_A:(2 * j + 2) * HD_A, :] = o_t[:, tq:2 * tq]


def _mla_tile(qt_ref, k_fn, vt_fn, n_chunks, kc, m_scr, l_scr, acc_scr, s_scr):
    tq = qt_ref.shape[1]

    def score_chunk(hh, c, slot, mx):
        s = _dot(k_fn(hh, c), qt_ref[hh * HEAD_PAD:(hh + 1) * HEAD_PAD, :])
        s_scr[slot, c * kc:(c + 1) * kc, 0:tq] = s
        cm = jnp.max(s, axis=0, keepdims=True)
        return cm if mx is None else jnp.maximum(mx, cm)

    ones_rows = _ones_rows(kc)
    mx = None
    for c in range(n_chunks):
        mx = score_chunk(0, c, 0, mx)
    for hh in range(H_B):
        slot = hh % 2
        m_prev = m_scr[hh]
        m_new = jnp.maximum(m_prev, mx)
        alpha = jnp.exp2(m_prev - m_new)
        m_scr[hh] = m_new
        pv = None
        mx = None
        for c in range(n_chunks):
            if hh + 1 < H_B:
                mx = score_chunk(hh + 1, c, 1 - slot, mx)
            p = jnp.exp2(s_scr[slot, c * kc:(c + 1) * kc, 0:tq] - m_new).astype(BF16)
            d = _dot(jnp.concatenate([vt_fn(hh, c), ones_rows], axis=0), p)
            pv = d if pv is None else pv + d
        l_scr[hh] = alpha * l_scr[hh] + pv[V_B:V_B + 1]
        rows = slice(hh * V_B, (hh + 1) * V_B)
        acc_scr[rows, :] = alpha * acc_scr[rows, :] + pv[:V_B]


def _mla_init(m_scr, l_scr, acc_scr):
    m_scr[...] = jnp.full(m_scr.shape, NEG_INF, F32)
    l_scr[...] = jnp.zeros(l_scr.shape, F32)
    acc_scr[...] = jnp.zeros(acc_scr.shape, F32)


def _mla_finish(o_ref, l_scr, acc_scr):
    for hh in range(H_B):
        rows = slice(hh * V_B, (hh + 1) * V_B)
        acc_scr[rows, :] = acc_scr[rows, :] / l_scr[hh]
    o_ref[...] = acc_scr[...].T.astype(o_ref.dtype)


def _win_kernel(sink_ref, qt_ref, kp_ref, kc_ref, kn_ref, kx_ref, vp_ref, vc_ref, vn_ref, vx_ref,
                o_ref, oT_scr, s_scr, bias_scr):
    n = pl.program_id(1)
    tq = qt_ref.shape[1]
    n_loc = tq + 2 * WINDOW
    col = lax.broadcasted_iota(jnp.int32, (n_loc, 2 * tq), 1)
    col = jnp.where(col >= tq, col - tq, col)
    r = lax.broadcasted_iota(jnp.int32, (n_loc, 2 * tq), 0)
    kabs = n * tq - WINDOW + r
    ok = (jnp.abs(r - WINDOW - col) <= WINDOW) & (kabs >= 0) & (kabs < SEQ)
    bias_scr[...] = jnp.where(ok, 0.0, NEG_INF)
    chunks = [(kp_ref, vp_ref, True), (kc_ref, vc_ref, True), (kn_ref, vn_ref, True),
              (kx_ref, vx_ref, False)]
    _gqa_heads(sink_ref, qt_ref, chunks, bias_scr, s_scr, oT_scr)
    o_ref[...] = oT_scr[...].T.astype(o_ref.dtype)


def _window_attn(sink, qat, ka, vat, ka_c, vat_c, batch):
    tq = TQ_WIN
    nq = SEQ // tq
    sub = tq // WINDOW
    nw = SEQ // WINDOW
    kv_w = HKV_A * HD_A
    prev_i = lambda b, n: b * nw + jnp.maximum(n * sub - 1, 0)
    next_i = lambda b, n: b * nw + jnp.minimum((n + 1) * sub, nw - 1)
    grid_spec = pltpu.PrefetchScalarGridSpec(
        num_scalar_prefetch=1,
        grid=(batch, nq),
        in_specs=[
            pl.BlockSpec((H_A * HD_A, tq), lambda b, n, s: (0, b * nq + n)),
            pl.BlockSpec((WINDOW, kv_w), lambda b, n, s: (prev_i(b, n), 0)),
            pl.BlockSpec((tq, kv_w), lambda b, n, s: (b * nq + n, 0)),
            pl.BlockSpec((WINDOW, kv_w), lambda b, n, s: (next_i(b, n), 0)),
            pl.BlockSpec((CTX_LEN, kv_w), lambda b, n, s: (b, 0)),
            pl.BlockSpec((kv_w, WINDOW), lambda b, n, s: (0, prev_i(b, n))),
            pl.BlockSpec((kv_w, tq), lambda b, n, s: (0, b * nq + n)),
            pl.BlockSpec((kv_w, WINDOW), lambda b, n, s: (0, next_i(b, n))),
            pl.BlockSpec((kv_w, CTX_LEN), lambda b, n, s: (0, b)),
        ],
        out_specs=pl.BlockSpec((tq, H_A * HD_A), lambda b, n, s: (b * nq + n, 0)),
        scratch_shapes=[pltpu.VMEM((H_A * HD_A, tq), F32),
                        _score_scratch(tq + 2 * WINDOW + CTX_LEN, 2 * tq),
                        pltpu.VMEM((tq + 2 * WINDOW, 2 * tq), F32)],
    )
    return pl.pallas_call(
        _win_kernel,
        grid_spec=grid_spec,
        out_shape=jax.ShapeDtypeStruct((batch * SEQ, H_A * HD_A), BF16),
        compiler_params=pltpu.CompilerParams(
            dimension_semantics=("arbitrary", "arbitrary"), vmem_limit_bytes=VMEM_LIMIT),
        name="window_attn",
    )(sink, qat, ka, ka, ka, ka_c, vat, vat, vat, vat_c)


def _mla_ctx_tile(qt_ref, kc_ref, vct_ref, m_scr, l_scr, acc_scr, s_scr):
    _mla_tile(qt_ref,
              lambda hh, c: kc_ref[:, hh * HEAD_PAD:(hh + 1) * HEAD_PAD],
              lambda hh, c: vct_ref[hh * V_B:(hh + 1) * V_B, :],
              1, kc_ref.shape[0], m_scr, l_scr, acc_scr, s_scr)


def _mla_kernel(qt_ref, kx_ref, vxt_ref, kc_ref, vct_ref, o_ref, m_scr, l_scr, acc_scr, s_scr):
    _mla_init(m_scr, l_scr, acc_scr)
    n_tiles = kx_ref.shape[0] // TK_MLA
    kc = KC_MLA

    def body(j, carry):
        def rows(c):
            return pl.ds(pl.multiple_of(j * TK_MLA + c * kc, kc), kc)

        _mla_tile(qt_ref,
                  lambda hh, c: kx_ref[rows(c), hh * HEAD_PAD:(hh + 1) * HEAD_PAD],
                  lambda hh, c: vxt_ref[hh * V_B:(hh + 1) * V_B, rows(c)],
                  TK_MLA // kc, kc, m_scr, l_scr, acc_scr, s_scr)
        return carry

    lax.fori_loop(0, n_tiles, body, 0)
    _mla_ctx_tile(qt_ref, kc_ref, vct_ref, m_scr, l_scr, acc_scr, s_scr)
    _mla_finish(o_ref, l_scr, acc_scr)


def _mla_scratch(tq, tk):
    return [pltpu.VMEM((H_B, 1, tq), F32), pltpu.VMEM((H_B, 1, tq), F32),
            pltpu.VMEM((H_B * V_B, tq), F32), _score_scratch(tk, tq)]


def _mla_attn(qbt, kb, vbt, kb_c, vbt_c, batch):
    tq = TQ_MLA
    nq = SEQ // tq
    res = lambda shape, fn: pl.BlockSpec(shape, fn, pipeline_mode=pl.Buffered(1))
    return pl.pallas_call(
        _mla_kernel,
        grid=(batch, nq),
        in_specs=[
            pl.BlockSpec((H_B * HEAD_PAD, tq), lambda b, i: (0, b * nq + i)),
            res((SEQ, H_B * HEAD_PAD), lambda b, i: (b, 0)),
            res((H_B * V_B, SEQ), lambda b, i: (0, b)),
            res((CTX_LEN, H_B * HEAD_PAD), lambda b, i: (b, 0)),
            res((H_B * V_B, CTX_LEN), lambda b, i: (0, b)),
        ],
        out_specs=pl.BlockSpec((tq, H_B * V_B), lambda b, i: (b * nq + i, 0)),
        out_shape=jax.ShapeDtypeStruct((batch * SEQ, H_B * V_B), BF16),
        scratch_shapes=_mla_scratch(tq, TK_MLA),
        compiler_params=pltpu.CompilerParams(
            dimension_semantics=("arbitrary", "arbitrary"), vmem_limit_bytes=VMEM_LIMIT),
        name="mla_attn",
    )(qbt, kb, vbt, kb_c, vbt_c)


def _ctx_attn_kernel(sink_ref, qat_ref, ka_ref, vat_ref, qbt_ref, kb_ref, vbt_ref,
                     ya_ref, yb_ref, oT_scr, m_scr, l_scr, acc_scr, s_scr):
    _gqa_heads(sink_ref, qat_ref, [(ka_ref, vat_ref, False)], None, s_scr, oT_scr)
    ya_ref[...] = oT_scr[...].T.astype(ya_ref.dtype)
    _mla_init(m_scr, l_scr, acc_scr)
    _mla_ctx_tile(qbt_ref, kb_ref, vbt_ref, m_scr, l_scr, acc_scr, s_scr)
    _mla_finish(yb_ref, l_scr, acc_scr)


def _ctx_attn(sink, qat_c, ka_c, vat_c, qbt_c, kb_c, vbt_c, batch):
    tok = lambda cols: pl.BlockSpec((CTX_LEN, cols), lambda b, s: (b, 0))
    feat = lambda rows: pl.BlockSpec((rows, CTX_LEN), lambda b, s: (0, b))
    grid_spec = pltpu.PrefetchScalarGridSpec(
        num_scalar_prefetch=1,
        grid=(batch,),
        in_specs=[feat(H_A * HD_A), tok(HKV_A * HD_A), feat(HKV_A * HD_A),
                  feat(H_B * HEAD_PAD), tok(H_B * HEAD_PAD), feat(H_B * V_B)],
        out_specs=[tok(H_A * HD_A), tok(H_B * V_B)],
        scratch_shapes=([pltpu.VMEM((H_A * HD_A, CTX_LEN), F32)] + _mla_scratch(CTX_LEN, CTX_LEN)[:3]
                        + [_score_scratch(CTX_LEN, 2 * CTX_LEN)]),
    )
    return pl.pallas_call(
        _ctx_attn_kernel,
        grid_spec=grid_spec,
        out_shape=[jax.ShapeDtypeStruct((batch * CTX_LEN, W_BRANCH), BF16)] * 2,
        compiler_params=pltpu.CompilerParams(
            dimension_semantics=("arbitrary",), vmem_limit_bytes=VMEM_LIMIT),
        name="ctx_attn",
    )(sink, qat_c, ka_c, vat_c, qbt_c, kb_c, vbt_c)


def _merge_kernel(ya_ref, yb_ref, sza_ref, szb_ref, szc_ref, bc_ref, cu_ref, cup_ref, cun_ref,
                  x_ref, shift_ref, scale_ref, gate_ref, gpre_ref, gpost_ref, cw_ref, wg_ref,
                  wbr_ref, wo_ref, o_ref, *, tiles_per_seq):
    i = pl.program_id(0)
    tm = x_ref.shape[0]
    x = x_ref[...]
    h = (_rms(x, gpre_ref[...]) * (1.0 + scale_ref[0]) + shift_ref[0]).astype(BF16)
    pos = i % tiles_per_seq
    z = cu_ref[...].astype(F32)
    prev_row = jnp.where(pos == 0, 0.0, cup_ref[7:8, :].astype(F32))
    next_row = jnp.where(pos == tiles_per_seq - 1, 0.0, cun_ref[0:1, :].astype(F32))
    row = lax.broadcasted_iota(jnp.int32, z.shape, 0)
    z_m1 = jnp.where(row == 0, prev_row, pltpu.roll(z, 1, 0))
    z_p1 = jnp.where(row == tm - 1, next_row, pltpu.roll(z, tm - 1, 0))
    conv = cw_ref[0:1, :] * z_m1 + cw_ref[1:2, :] * z + cw_ref[2:3, :] * z_p1
    yc = bc_ref[...].astype(F32) * conv

    def branch(y, sz_ref, k):
        t = (y * sz_ref[...].astype(F32)).astype(BF16)
        sg = jax.nn.sigmoid(_dot(h, wg_ref[:, k * D_MODEL:(k + 1) * D_MODEL]))
        return sg * _dot(t, wbr_ref[k])

    m = (branch(ya_ref[...].astype(F32), sza_ref, 0)
         + branch(yb_ref[...].astype(F32), szb_ref, 1)
         + branch(yc, szc_ref, 2))
    out = _dot(m.astype(BF16), wo_ref[...])
    o_ref[...] = x + gate_ref[0] * _rms(out, gpost_ref[...])


def _merge(ya, yb, fr, resid, mod3, mod_row_of_seq, gpre, gpost, conv_w, wg, wbr, wo,
           tiles_per_seq_rows):
    (_, _, _, sza, _, _, _, szb, bc, cu, szc) = fr
    t_rows = resid.shape[0]
    tm = min(TM_MERGE, tiles_per_seq_rows)
    tiles_per_seq = tiles_per_seq_rows // tm
    n_tiles = t_rows // tm
    sub = tm // 8
    n_sub = t_rows // 8
    mod_row_fn = lambda i: mod_row_of_seq(i // tiles_per_seq)
    row = lambda cols: pl.BlockSpec((tm, cols), lambda i: (i, 0))
    const = lambda shape: pl.BlockSpec(shape, lambda i: (0,) * len(shape),
                                       pipeline_mode=pl.Buffered(1))
    in_specs = [
        row(512), row(512), row(512), row(512), row(512), row(512), row(512),
        pl.BlockSpec((8, 512), lambda i: (jnp.maximum(i * sub - 1, 0), 0)),
        pl.BlockSpec((8, 512), lambda i: (jnp.minimum((i + 1) * sub, n_sub - 1), 0)),
        row(1024),
        pl.BlockSpec((1, 1, D_MODEL), lambda i: (mod_row_fn(i), 0, 0)),
        pl.BlockSpec((1, 1, D_MODEL), lambda i: (mod_row_fn(i), 0, 1)),
        pl.BlockSpec((1, 1, D_MODEL), lambda i: (mod_row_fn(i), 0, 2)),
        const((1, D_MODEL)),
        const((1, D_MODEL)),
        const((3, W_C)),
        const((D_MODEL, 3 * D_MODEL)),
        const((3, W_BRANCH, D_MODEL)),
        const((D_MODEL, D_MODEL)),
    ]
    return pl.pallas_call(
        functools.partial(_merge_kernel, tiles_per_seq=tiles_per_seq),
        grid=(n_tiles,),
        in_specs=in_specs,
        out_specs=pl.BlockSpec((tm, D_MODEL), lambda i: (i, 0)),
        out_shape=jax.ShapeDtypeStruct((t_rows, D_MODEL), F32),
        compiler_params=pltpu.CompilerParams(
            dimension_semantics=("arbitrary",), vmem_limit_bytes=VMEM_LIMIT),
        name="merge",
    )(ya, yb, sza, szb, szc, bc, cu, cu, cu, resid, mod3, mod3, mod3, gpre, gpost, conv_w, wg,
      wbr, wo)


def _pack_w_in(w):
    o = [int(v) for v in np.concatenate([[0], np.cumsum(IN_SIZES)])]
    wb = w.astype(BF16)
    krp = jnp.pad(wb[:, o[6]:o[7]], ((0, 0), (NOPE_B, HEAD_PAD - NOPE_B - ROPE_B)))
    w2 = jnp.concatenate([wb[:, o[1]:o[2]], wb[:, o[3]:o[6]], krp, wb[:, o[7]:o[12]]], axis=1)
    wt = jnp.concatenate([wb[:, o[0]:o[1]], wb[:, o[2]:o[3]]], axis=1).T
    wg = wb[:, o[12]:o[15]]
    return w2, wt, wg


def _pack_w_qb(w):
    w3 = w.reshape(Q_LORA, H_B, NOPE_B + ROPE_B)
    w3 = jnp.pad(w3, ((0, 0), (0, 0), (0, HEAD_PAD - NOPE_B - ROPE_B)))
    return w3.reshape(Q_LORA, H_B * HEAD_PAD).T.astype(BF16)


def _pack_w_kvb(w):
    w3 = w.reshape(KV_LORA, H_B, NOPE_B + V_B)
    wkn = jnp.pad(w3[..., :NOPE_B], ((0, 0), (0, 0), (0, HEAD_PAD - NOPE_B)))
    wv = w3[..., NOPE_B:]
    return (wkn.reshape(KV_LORA, H_B * HEAD_PAD).astype(BF16),
            wv.reshape(KV_LORA, H_B * V_B).T.astype(BF16))


def _rope_tables(seq):
    t = jnp.arange(seq)
    row = (t // GRID_W).astype(F32)
    col = (t % GRID_W).astype(F32)

    def angles(rot_dim):
        axis_dim = rot_dim // 2
        inv = ROPE_BASE ** (-jnp.arange(0, axis_dim, 2, dtype=F32) / axis_dim)
        return jnp.concatenate([row[:, None] * inv, col[:, None] * inv], axis=-1)

    a = angles(HD_A)
    ca, sa = jnp.cos(a), jnp.sin(a)
    cos_a = jnp.tile(jnp.concatenate([ca, ca], axis=-1), (1, LANES // HD_A))
    sin_a = jnp.tile(jnp.concatenate([-sa, sa], axis=-1), (1, LANES // HD_A))
    b = angles(ROPE_B)
    cb, sb = jnp.cos(b), jnp.sin(b)
    tail = HEAD_PAD - NOPE_B - ROPE_B
    cos_b = jnp.concatenate([jnp.ones((seq, NOPE_B), F32), cb, cb, jnp.ones((seq, tail), F32)], axis=-1)
    sin_b = jnp.concatenate([jnp.zeros((seq, NOPE_B), F32), -sb, sb, jnp.zeros((seq, tail), F32)], axis=-1)
    return cos_a, sin_a, cos_b, sin_b, ca.T, sa.T, cb.T, sb.T


def kernel(x, c, ctx, c_ctx, w_mod, b_mod, g_pre, g_post, w_in, sink, g_qa, w_qb, g_kva, w_kvb,
           conv_w, w_branch, w_o):
    batch, seq, d = x.shape
    assert (seq, d) == (SEQ, D_MODEL) and ctx.shape == (batch, CTX_LEN, D_MODEL)
    assert batch <= 6
    depth = w_mod.shape[0]

    c_rows = jnp.concatenate([c, c_ctx[None, :], jnp.zeros((8 - batch - 1, d), F32)], axis=0)
    mod_all = _modulation(c_rows, w_mod, b_mod)
    tables = _rope_tables(seq)

    x2 = x.reshape(batch * seq, d)
    ctx2 = ctx.reshape(batch * CTX_LEN, d)
    x_tiles_front = seq // TM_FRONT

    for i in range(depth):
        mod3 = mod_all[i].reshape(8, 1, 3 * d)
        w2, wt, wg = _pack_w_in(w_in[i])
        wqbt = _pack_w_qb(w_qb[i])
        wkn, wvt = _pack_w_kvb(w_kvb[i])
        gpre = g_pre[i].reshape(1, d)
        gqa = g_qa[i].reshape(1, Q_LORA)
        gkva = g_kva[i].reshape(1, KV_LORA)
        gpost = g_post[i].reshape(1, d)
        wbr = w_branch[i].astype(BF16)
        wo = w_o[i].astype(BF16)

        fx = _front(x2, mod3, lambda t: t // x_tiles_front, gpre, w2, wt, gqa, wqbt, gkva, wkn,
                    wvt, tables, x_tiles_front)
        fc = _front(ctx2, mod3, lambda t: batch, gpre, w2, wt, gqa, wqbt, gkva, wkn, wvt, None, 1)
        qat, ka, vat, _, qbt, kb, vbt = fx[:7]
        qat_c, ka_c, vat_c, _, qbt_c, kb_c, vbt_c = fc[:7]

        ya = _window_attn(sink[i], qat, ka, vat, ka_c, vat_c, batch)
        yb = _mla_attn(qbt, kb, vbt, kb_c, vbt_c, batch)
        x2_new = _merge(ya, yb, fx, x2, mod3, lambda b: b, gpre, gpost,
                        conv_w[i], wg, wbr, wo, seq)
        if i < depth - 1:
            ya_c, yb_c = _ctx_attn(sink[i], qat_c, ka_c, vat_c, qbt_c, kb_c, vbt_c, batch)
            ctx2 = _merge(ya_c, yb_c, fc, ctx2, mod3, lambda b: batch, gpre, gpost,
                          conv_w[i], wg, wbr, wo, CTX_LEN)
        x2 = x2_new
    return x2.reshape(batch, seq, d)
```

```python
import functools

import jax
import jax.numpy as jnp
import numpy as np
from jax import lax
from jax.experimental import pallas as pl
from jax.experimental.pallas import tpu as pltpu

F32 = jnp.float32
BF16 = jnp.bfloat16

D_MODEL = 1024
SEQ = 8192
DEPTH = 2
CTX_LEN = 256
GRID_W = 64
WINDOW = 128
ROPE_BASE = 10000.0
EPS = 1e-6
NEG_INF = -1e30
LOG2E = float(np.log2(np.e))

H_A = 8
HKV_A = 2
G_A = H_A // HKV_A
HD_A = 64
QA_SCALE = HD_A ** -0.5 * LOG2E

H_B = 8
Q_LORA = 384
KV_LORA = 256
NOPE_B = 64
ROPE_B = 32
V_B = 64
QB_SCALE = (NOPE_B + ROPE_B) ** -0.5 * LOG2E

W_C = 512
W_BRANCH = 512

IN_SIZES = (H_A * HD_A, HKV_A * HD_A, HKV_A * HD_A, W_BRANCH,
            Q_LORA, KV_LORA, ROPE_B, W_BRANCH,
            W_C, W_C, W_C, W_BRANCH,
            D_MODEL, D_MODEL, D_MODEL)

LANES = 128
HEAD_PAD = 128

C_KA = 0
C_ZA = 128
C_QL = 640
C_KVL = 1024
C_KR = 1280
C_ZB = 1408
C_BC = 1920
C_CC = 2432
C_UC = 2944
C_ZC = 3456
C_END = 3968
R_QA = 0
R_VA = H_A * HD_A
R_END = R_VA + HKV_A * HD_A

VMEM_LIMIT = 56 * 1024 * 1024

TM_FRONT = 1024
TM_MERGE = 512
TQ_WIN = 256
TQ_MLA = 512
TK_MLA = 4096
KC_MLA = 512
SCORE_PAD = LANES


def _silu(v):
    return v * jax.nn.sigmoid(v)


def _rms(v, g):
    return v * lax.rsqrt(jnp.mean(v * v, axis=-1, keepdims=True) + EPS) * g


def _dot(a, b):
    return jnp.dot(a, b, preferred_element_type=F32)


def _dot_t(a, b):
    return lax.dot_general(a, b, (((1,), (1,)), ((), ())), preferred_element_type=F32)


def _mod_kernel(c_ref, w_ref, b_ref, o_ref):
    sc = _silu(c_ref[...])
    o_ref[0] = jnp.dot(sc, w_ref[0], preferred_element_type=F32,
                       precision=lax.Precision.HIGHEST) + b_ref[0]


def _modulation(c_rows, w_mod, b_mod):
    n_chunk = 3
    return pl.pallas_call(
        _mod_kernel,
        grid=(DEPTH, n_chunk),
        in_specs=[
            pl.BlockSpec((8, D_MODEL), lambda l, j: (0, 0)),
            pl.BlockSpec((1, D_MODEL, D_MODEL), lambda l, j: (l, 0, j)),
            pl.BlockSpec((1, 1, D_MODEL), lambda l, j: (l, 0, j)),
        ],
        out_specs=pl.BlockSpec((1, 8, D_MODEL), lambda l, j: (l, 0, j)),
        out_shape=jax.ShapeDtypeStruct((DEPTH, 8, 3 * D_MODEL), F32),
        compiler_params=pltpu.CompilerParams(
            dimension_semantics=("arbitrary", "arbitrary"), vmem_limit_bytes=VMEM_LIMIT),
        name="modulation",
    )(c_rows, w_mod, b_mod.reshape(DEPTH, 1, 3 * D_MODEL))


def _rope_lanes(v, cos, sin, first_half, up_shift, dn_shift):
    rot = jnp.where(first_half, pltpu.roll(v, up_shift, 1), pltpu.roll(v, dn_shift, 1))
    return v * cos + rot * sin


def _rope_rows(x1, x2, cos, sin):
    return x1 * cos - x2 * sin, x2 * cos + x1 * sin


def _front_kernel(x_ref, shift_ref, scale_ref, gpre_ref, w_ref, wt_ref, gqa_ref, wqbt_ref,
                  gkva_ref, wkn_ref, wvt_ref, *rest, rope):
    if rope:
        (cosa_ref, sina_ref, cosb_ref, sinb_ref,
         cosat_ref, sinat_ref, cosbt_ref, sinbt_ref) = rest[:8]
        outs = rest[8:]
    else:
        outs = rest
    (qat_ref, ka_ref, vat_ref, sza_ref, qbt_ref, kb_ref, vbt_ref, szb_ref,
     bc_ref, cu_ref, szc_ref) = outs

    x = x_ref[...]
    h = _rms(x, gpre_ref[...]) * (1.0 + scale_ref[0]) + shift_ref[0]
    h = h.astype(BF16)

    def mm(a, b):
        return _dot(h, w_ref[:, a:b])

    qv_t = _dot_t(wt_ref[...], h)
    half = HD_A // 2
    for hh in range(H_A):
        r0 = R_QA + hh * HD_A
        x1, x2 = qv_t[r0:r0 + half], qv_t[r0 + half:r0 + HD_A]
        if rope:
            x1, x2 = _rope_rows(x1, x2, cosat_ref[...], sinat_ref[...])
        qat_ref[hh * HD_A:hh * HD_A + half, :] = (x1 * QA_SCALE).astype(BF16)
        qat_ref[hh * HD_A + half:(hh + 1) * HD_A, :] = (x2 * QA_SCALE).astype(BF16)
    vat_ref[...] = qv_t[R_VA:R_END].astype(BF16)
    ka = mm(C_KA, C_ZA)
    if rope:
        lane = lax.broadcasted_iota(jnp.int32, (1, LANES), 1)
        ka = _rope_lanes(ka, cosa_ref[...], sina_ref[...], (lane % HD_A) < half,
                         LANES - half, half)
    ka_ref[...] = ka.astype(BF16)
    sza_ref[...] = _silu(mm(C_ZA, C_QL)).astype(BF16)

    qn = _rms(mm(C_QL, C_KVL), gqa_ref[...]).astype(BF16)
    qb_t = _dot_t(wqbt_ref[...], qn)
    hr = ROPE_B // 2
    for hh in range(H_B):
        r0 = hh * HEAD_PAD
        x1 = qb_t[r0 + NOPE_B:r0 + NOPE_B + hr]
        x2 = qb_t[r0 + NOPE_B + hr:r0 + NOPE_B + ROPE_B]
        if rope:
            x1, x2 = _rope_rows(x1, x2, cosbt_ref[...], sinbt_ref[...])
        qbt_ref[r0:r0 + NOPE_B, :] = (qb_t[r0:r0 + NOPE_B] * QB_SCALE).astype(BF16)
        qbt_ref[r0 + NOPE_B:r0 + NOPE_B + hr, :] = (x1 * QB_SCALE).astype(BF16)
        qbt_ref[r0 + NOPE_B + hr:r0 + NOPE_B + ROPE_B, :] = (x2 * QB_SCALE).astype(BF16)
        qbt_ref[r0 + NOPE_B + ROPE_B:r0 + HEAD_PAD, :] = jnp.zeros(
            (HEAD_PAD - NOPE_B - ROPE_B, x.shape[0]), BF16)
    kvn = _rms(mm(C_KVL, C_KR), gkva_ref[...]).astype(BF16)
    kn = _dot(kvn, wkn_ref[...])
    kr = mm(C_KR, C_ZB)
    if rope:
        kr = _rope_lanes(kr, cosb_ref[...], sinb_ref[...], lane < (NOPE_B + hr),
                         LANES - hr, hr)
    for hh in range(H_B):
        sl = slice(hh * HEAD_PAD, (hh + 1) * HEAD_PAD)
        kb_ref[:, sl] = (kn[:, sl] + kr).astype(BF16)
    vbt_ref[...] = _dot_t(wvt_ref[...], kvn).astype(BF16)
    szb_ref[...] = _silu(mm(C_ZB, C_BC)).astype(BF16)

    bc_ref[...] = mm(C_BC, C_CC).astype(BF16)
    cu_ref[...] = (mm(C_CC, C_UC) * mm(C_UC, C_ZC)).astype(BF16)
    szc_ref[...] = _silu(mm(C_ZC, C_END)).astype(BF16)


_FRONT_OUTS = ((True, H_A * HD_A), (False, HKV_A * HD_A), (True, HKV_A * HD_A), (False, W_BRANCH),
               (True, H_B * HEAD_PAD), (False, H_B * HEAD_PAD), (True, H_B * V_B), (False, W_BRANCH),
               (False, W_C), (False, W_C), (False, W_BRANCH))


def _front(rows, mod3, mod_row_fn, gpre, w2, wt, gqa, wqbt, gkva, wkn, wvt, tables, tiles_per_seq):
    t_rows = rows.shape[0]
    tm = min(TM_FRONT, t_rows)
    n_tiles = t_rows // tm
    rope = tables is not None
    const = lambda shape: pl.BlockSpec(shape, lambda i: (0,) * len(shape),
                                       pipeline_mode=pl.Buffered(1))
    in_specs = [
        pl.BlockSpec((tm, D_MODEL), lambda i: (i, 0)),
        pl.BlockSpec((1, 1, D_MODEL), lambda i: (mod_row_fn(i), 0, 0)),
        pl.BlockSpec((1, 1, D_MODEL), lambda i: (mod_row_fn(i), 0, 1)),
        const((1, D_MODEL)),
        const((D_MODEL, C_END)),
        const((R_END, D_MODEL)),
        const((1, Q_LORA)),
        const((H_B * HEAD_PAD, Q_LORA)),
        const((1, KV_LORA)),
        const((KV_LORA, H_B * HEAD_PAD)),
        const((H_B * V_B, KV_LORA)),
    ]
    args = [rows, mod3, mod3, gpre, w2, wt, gqa, wqbt, gkva, wkn, wvt]
    if rope:
        pos = lambda i: i % tiles_per_seq
        in_specs += [pl.BlockSpec((tm, LANES), lambda i: (pos(i), 0))] * 4
        in_specs += [pl.BlockSpec((HD_A // 2, tm), lambda i: (0, pos(i)))] * 2
        in_specs += [pl.BlockSpec((ROPE_B // 2, tm), lambda i: (0, pos(i)))] * 2
        args += list(tables)
    out_specs, out_shape = [], []
    for transposed, width in _FRONT_OUTS:
        if transposed:
            out_specs.append(pl.BlockSpec((width, tm), lambda i: (0, i)))
            out_shape.append(jax.ShapeDtypeStruct((width, t_rows), BF16))
        else:
            out_specs.append(pl.BlockSpec((tm, width), lambda i: (i, 0)))
            out_shape.append(jax.ShapeDtypeStruct((t_rows, width), BF16))
    return pl.pallas_call(
        functools.partial(_front_kernel, rope=rope),
        grid=(n_tiles,),
        in_specs=in_specs,
        out_specs=out_specs,
        out_shape=out_shape,
        compiler_params=pltpu.CompilerParams(
            dimension_semantics=("arbitrary",), vmem_limit_bytes=VMEM_LIMIT),
        name="front_rope" if rope else "front_ctx",
    )(*args)


def _score_scratch(nk, tq):
    return pltpu.VMEM((2, nk, tq + SCORE_PAD), F32)


def _ones_rows(kc):
    return (lax.broadcasted_iota(jnp.int32, (16, kc), 0) == 0).astype(BF16)


def _gqa_heads(sink_ref, qt_ref, chunks, bias_scr, s_scr, oT_scr):
    tq = qt_ref.shape[1]
    sizes = [k_ref.shape[0] for k_ref, _, _ in chunks]
    offs = [sum(sizes[:c]) for c in range(len(chunks))]
    n_chunks = len(chunks)
    n_pairs = H_A // 2
    zeros = jnp.zeros((HD_A, tq), BF16)
    ones_rows = {kc: _ones_rows(kc) for kc in set(sizes)}
    in_first = lax.broadcasted_iota(jnp.int32, (1, 2 * tq), 1) < tq

    def weights(j):
        cols = []
        for hh in (2 * j, 2 * j + 1):
            q_t = qt_ref[hh * HD_A:(hh + 1) * HD_A, :]
            cols.append(jnp.concatenate([q_t, zeros] if hh // G_A == 0 else [zeros, q_t], axis=0))
        return jnp.concatenate(cols, axis=1)

    def score_chunk(w, c, slot, mx):
        k_ref, _, masked = chunks[c]
        rows = slice(offs[c], offs[c] + sizes[c])
        s = _dot(k_ref[...], w)
        if masked:
            s = s + bias_scr[rows, :]
        s_scr[slot, rows, 0:2 * tq] = s
        cm = jnp.max(s, axis=0, keepdims=True)
        return cm if mx is None else jnp.maximum(mx, cm)

    mx = None
    w = weights(0)
    for c in range(n_chunks):
        mx = score_chunk(w, c, 0, mx)
    for j in range(n_pairs):
        g = (2 * j) // G_A
        slot = j % 2
        sink = jnp.where(in_first, sink_ref[2 * j] * LOG2E, sink_ref[2 * j + 1] * LOG2E)
        m = jnp.maximum(mx, sink)
        pv = None
        mx = None
        if j + 1 < n_pairs:
            w = weights(j + 1)
        for c in range(n_chunks):
            if j + 1 < n_pairs:
                mx = score_chunk(w, c, 1 - slot, mx)
            rows = slice(offs[c], offs[c] + sizes[c])
            p = jnp.exp2(s_scr[slot, rows, 0:2 * tq] - m).astype(BF16)
            vt = jnp.concatenate([chunks[c][1][g * HD_A:(g + 1) * HD_A, :], ones_rows[sizes[c]]],
                                 axis=0)
            d = _dot(vt, p)
            pv = d if pv is None else pv + d
        o_t = pv[:HD_A] / (pv[HD_A:HD_A + 1] + jnp.exp2(sink - m))
        oT_scr[2 * j * HD_A:(2 * j + 1) * HD_A, :] = o_t[:, 0:tq]
        oT_scr[(2 * j + 1) * HD_A:(2 * j + 2) * HD_A, :] = o_t[:, tq:2 * tq]


def _mla_pipeline(qt_ref, steps, m_scr, l_scr, acc_scr, s_scr):
    tq = qt_ref.shape[1]
    ones_rows = {size: _ones_rows(size) for _, chunks in steps for _, _, size in chunks}

    def offsets(chunks):
        return [sum(size for _, _, size in chunks[:c]) for c in range(len(chunks))]

    def score_chunk(step, c, slot, mx):
        hh, chunks = step
        k_fn, _, size = chunks[c]
        off = offsets(chunks)[c]
        s = _dot(k_fn(), qt_ref[hh * HEAD_PAD:(hh + 1) * HEAD_PAD, :])
        s_scr[slot, off:off + size, 0:tq] = s
        cm = jnp.max(s, axis=0, keepdims=True)
        return cm if mx is None else jnp.maximum(mx, cm)

    mx = None
    for c in range(len(steps[0][1])):
        mx = score_chunk(steps[0], c, 0, mx)
    for t, (hh, chunks) in enumerate(steps):
        slot = t % 2
        nxt = steps[t + 1] if t + 1 < len(steps) else None
        n_nxt = len(nxt[1]) if nxt is not None else 0
        offs = offsets(chunks)
        m_prev = m_scr[hh]
        m_new = jnp.maximum(m_prev, mx)
        alpha = jnp.exp2(m_prev - m_new)
        m_scr[hh] = m_new
        pv = None
        mx = None
        for c in range(max(len(chunks), n_nxt)):
            if c < n_nxt:
                mx = score_chunk(nxt, c, 1 - slot, mx)
            if c < len(chunks):
                _, vt_fn, size = chunks[c]
                p = jnp.exp2(s_scr[slot, offs[c]:offs[c] + size, 0:tq] - m_new).astype(BF16)
                d = _dot(jnp.concatenate([vt_fn(), ones_rows[size]], axis=0), p)
                pv = d if pv is None else pv + d
        l_scr[hh] = alpha * l_scr[hh] + pv[V_B:V_B + 1]
        rows = slice(hh * V_B, (hh + 1) * V_B)
        acc_scr[rows, :] = alpha * acc_scr[rows, :] + pv[:V_B]


def _mla_steps(k_ref, vt_ref, row0, sizes):
    def span(a, n):
        return slice(a, a + n) if isinstance(a, int) else pl.ds(pl.multiple_of(a, LANES), n)

    steps = []
    for hh in range(H_B):
        chunks, r = [], row0
        for size in sizes:
            chunks.append((
                functools.partial(lambda a, n, h: k_ref[span(a, n), h * HEAD_PAD:(h + 1) * HEAD_PAD],
                                  r, size, hh),
                functools.partial(lambda a, n, h: vt_ref[h * V_B:(h + 1) * V_B, span(a, n)],
                                  r, size, hh),
                size))
            r = r + size
        steps.append((hh, chunks))
    return steps


def _mla_init(m_scr, l_scr, acc_scr):
    m_scr[...] = jnp.full(m_scr.shape, NEG_INF, F32)
    l_scr[...] = jnp.zeros(l_scr.shape, F32)
    acc_scr[...] = jnp.zeros(acc_scr.shape, F32)


def _mla_finish(o_ref, l_scr, acc_scr):
    for hh in range(H_B):
        rows = slice(hh * V_B, (hh + 1) * V_B)
        acc_scr[rows, :] = acc_scr[rows, :] / l_scr[hh]
    o_ref[...] = acc_scr[...].T.astype(o_ref.dtype)


def _win_kernel(sink_ref, qt_ref, kp_ref, kc_ref, kn_ref, kx_ref, vp_ref, vc_ref, vn_ref, vx_ref,
                o_ref, oT_scr, s_scr, bias_scr):
    n = pl.program_id(1)
    tq = qt_ref.shape[1]
    n_loc = tq + 2 * WINDOW
    col = lax.broadcasted_iota(jnp.int32, (n_loc, 2 * tq), 1)
    col = jnp.where(col >= tq, col - tq, col)
    r = lax.broadcasted_iota(jnp.int32, (n_loc, 2 * tq), 0)
    kabs = n * tq - WINDOW + r
    ok = (jnp.abs(r - WINDOW - col) <= WINDOW) & (kabs >= 0) & (kabs < SEQ)
    bias_scr[...] = jnp.where(ok, 0.0, NEG_INF)
    chunks = [(kp_ref, vp_ref, True), (kc_ref, vc_ref, True), (kn_ref, vn_ref, True),
              (kx_ref, vx_ref, False)]
    _gqa_heads(sink_ref, qt_ref, chunks, bias_scr, s_scr, oT_scr)
    o_ref[...] = oT_scr[...].T.astype(o_ref.dtype)


def _window_attn(sink, qat, ka, vat, ka_c, vat_c, batch):
    tq = TQ_WIN
    nq = SEQ // tq
    sub = tq // WINDOW
    nw = SEQ // WINDOW
    kv_w = HKV_A * HD_A
    prev_i = lambda b, n: b * nw + jnp.maximum(n * sub - 1, 0)
    next_i = lambda b, n: b * nw + jnp.minimum((n + 1) * sub, nw - 1)
    grid_spec = pltpu.PrefetchScalarGridSpec(
        num_scalar_prefetch=1,
        grid=(batch, nq),
        in_specs=[
            pl.BlockSpec((H_A * HD_A, tq), lambda b, n, s: (0, b * nq + n)),
            pl.BlockSpec((WINDOW, kv_w), lambda b, n, s: (prev_i(b, n), 0)),
            pl.BlockSpec((tq, kv_w), lambda b, n, s: (b * nq + n, 0)),
            pl.BlockSpec((WINDOW, kv_w), lambda b, n, s: (next_i(b, n), 0)),
            pl.BlockSpec((CTX_LEN, kv_w), lambda b, n, s: (b, 0)),
            pl.BlockSpec((kv_w, WINDOW), lambda b, n, s: (0, prev_i(b, n))),
            pl.BlockSpec((kv_w, tq), lambda b, n, s: (0, b * nq + n)),
            pl.BlockSpec((kv_w, WINDOW), lambda b, n, s: (0, next_i(b, n))),
            pl.BlockSpec((kv_w, CTX_LEN), lambda b, n, s: (0, b)),
        ],
        out_specs=pl.BlockSpec((tq, H_A * HD_A), lambda b, n, s: (b * nq + n, 0)),
        scratch_shapes=[pltpu.VMEM((H_A * HD_A, tq), F32),
                        _score_scratch(tq + 2 * WINDOW + CTX_LEN, 2 * tq),
                        pltpu.VMEM((tq + 2 * WINDOW, 2 * tq), F32)],
    )
    return pl.pallas_call(
        _win_kernel,
        grid_spec=grid_spec,
        out_shape=jax.ShapeDtypeStruct((batch * SEQ, H_A * HD_A), BF16),
        compiler_params=pltpu.CompilerParams(
            dimension_semantics=("arbitrary", "arbitrary"), vmem_limit_bytes=VMEM_LIMIT),
        name="window_attn",
    )(sink, qat, ka, ka, ka, ka_c, vat, vat, vat, vat_c)


def _mla_kernel(qt_ref, kx_ref, vxt_ref, kc_ref, vct_ref, o_ref, m_scr, l_scr, acc_scr, s_scr):
    _mla_init(m_scr, l_scr, acc_scr)
    n_tiles = kx_ref.shape[0] // TK_MLA

    def body(j, carry):
        steps = _mla_steps(kx_ref, vxt_ref, j * TK_MLA, [KC_MLA] * (TK_MLA // KC_MLA))
        _mla_pipeline(qt_ref, steps, m_scr, l_scr, acc_scr, s_scr)
        return carry

    lax.fori_loop(0, n_tiles, body, 0)
    _mla_pipeline(qt_ref, _mla_steps(kc_ref, vct_ref, 0, [kc_ref.shape[0]]),
                  m_scr, l_scr, acc_scr, s_scr)
    _mla_finish(o_ref, l_scr, acc_scr)


def _mla_scratch(tq, tk):
    return [pltpu.VMEM((H_B, 1, tq), F32), pltpu.VMEM((H_B, 1, tq), F32),
            pltpu.VMEM((H_B * V_B, tq), F32), _score_scratch(tk, tq)]


def _mla_attn(qbt, kb, vbt, kb_c, vbt_c, batch):
    tq = TQ_MLA
    nq = SEQ // tq
    res = lambda shape, fn: pl.BlockSpec(shape, fn, pipeline_mode=pl.Buffered(1))
    return pl.pallas_call(
        _mla_kernel,
        grid=(batch, nq),
        in_specs=[
            pl.BlockSpec((H_B * HEAD_PAD, tq), lambda b, i: (0, b * nq + i)),
            res((SEQ, H_B * HEAD_PAD), lambda b, i: (b, 0)),
            res((H_B * V_B, SEQ), lambda b, i: (0, b)),
            res((CTX_LEN, H_B * HEAD_PAD), lambda b, i: (b, 0)),
            res((H_B * V_B, CTX_LEN), lambda b, i: (0, b)),
        ],
        out_specs=pl.BlockSpec((tq, H_B * V_B), lambda b, i: (b * nq + i, 0)),
        out_shape=jax.ShapeDtypeStruct((batch * SEQ, H_B * V_B), BF16),
        scratch_shapes=_mla_scratch(tq, TK_MLA),
        compiler_params=pltpu.CompilerParams(
            dimension_semantics=("arbitrary", "arbitrary"), vmem_limit_bytes=VMEM_LIMIT),
        name="mla_attn",
    )(qbt, kb, vbt, kb_c, vbt_c)


def _ctx_attn_kernel(sink_ref, qat_ref, ka_ref, vat_ref, qbt_ref, kb_ref, vbt_ref,
                     ya_ref, yb_ref, oT_scr, m_scr, l_scr, acc_scr, s_scr):
    _gqa_heads(sink_ref, qat_ref, [(ka_ref, vat_ref, False)], None, s_scr, oT_scr)
    ya_ref[...] = oT_scr[...].T.astype(ya_ref.dtype)
    _mla_init(m_scr, l_scr, acc_scr)
    _mla_pipeline(qbt_ref, _mla_steps(kb_ref, vbt_ref, 0, [kb_ref.shape[0]]),
                  m_scr, l_scr, acc_scr, s_scr)
    _mla_finish(yb_ref, l_scr, acc_scr)


def _ctx_attn(sink, qat_c, ka_c, vat_c, qbt_c, kb_c, vbt_c, batch):
    tok = lambda cols: pl.BlockSpec((CTX_LEN, cols), lambda b, s: (b, 0))
    feat = lambda rows: pl.BlockSpec((rows, CTX_LEN), lambda b, s: (0, b))
    grid_spec = pltpu.PrefetchScalarGridSpec(
        num_scalar_prefetch=1,
        grid=(batch,),
        in_specs=[feat(H_A * HD_A), tok(HKV_A * HD_A), feat(HKV_A * HD_A),
                  feat(H_B * HEAD_PAD), tok(H_B * HEAD_PAD), feat(H_B * V_B)],
        out_specs=[tok(H_A * HD_A), tok(H_B * V_B)],
        scratch_shapes=([pltpu.VMEM((H_A * HD_A, CTX_LEN), F32)] + _mla_scratch(CTX_LEN, CTX_LEN)[:3]
                        + [_score_scratch(CTX_LEN, 2 * CTX_LEN)]),
    )
    return pl.pallas_call(
        _ctx_attn_kernel,
        grid_spec=grid_spec,
        out_shape=[jax.ShapeDtypeStruct((batch * CTX_LEN, W_BRANCH), BF16)] * 2,
        compiler_params=pltpu.CompilerParams(
            dimension_semantics=("arbitrary",), vmem_limit_bytes=VMEM_LIMIT),
        name="ctx_attn",
    )(sink, qat_c, ka_c, vat_c, qbt_c, kb_c, vbt_c)


def _merge_kernel(ya_ref, yb_ref, sza_ref, szb_ref, szc_ref, bc_ref, cu_ref, cup_ref, cun_ref,
                  x_ref, shift_ref, scale_ref, gate_ref, gpre_ref, gpost_ref, cw_ref, wg_ref,
                  wbr_ref, wo_ref, o_ref, *, tiles_per_seq):
    i = pl.program_id(0)
    tm = x_ref.shape[0]
    x = x_ref[...]
    h = (_rms(x, gpre_ref[...]) * (1.0 + scale_ref[0]) + shift_ref[0]).astype(BF16)
    pos = i % tiles_per_seq
    z = cu_ref[...].astype(F32)
    prev_row = jnp.where(pos == 0, 0.0, cup_ref[7:8, :].astype(F32))
    next_row = jnp.where(pos == tiles_per_seq - 1, 0.0, cun_ref[0:1, :].astype(F32))
    row = lax.broadcasted_iota(jnp.int32, z.shape, 0)
    z_m1 = jnp.where(row == 0, prev_row, pltpu.roll(z, 1, 0))
    z_p1 = jnp.where(row == tm - 1, next_row, pltpu.roll(z, tm - 1, 0))
    conv = cw_ref[0:1, :] * z_m1 + cw_ref[1:2, :] * z + cw_ref[2:3, :] * z_p1
    yc = bc_ref[...].astype(F32) * conv

    def branch(y, sz_ref, k):
        t = (y * sz_ref[...].astype(F32)).astype(BF16)
        sg = jax.nn.sigmoid(_dot(h, wg_ref[:, k * D_MODEL:(k + 1) * D_MODEL]))
        return sg * _dot(t, wbr_ref[k])

    m = (branch(ya_ref[...].astype(F32), sza_ref, 0)
         + branch(yb_ref[...].astype(F32), szb_ref, 1)
         + branch(yc, szc_ref, 2))
    out = _dot(m.astype(BF16), wo_ref[...])
    o_ref[...] = x + gate_ref[0] * _rms(out, gpost_ref[...])


def _merge(ya, yb, fr, resid, mod3, mod_row_of_seq, gpre, gpost, conv_w, wg, wbr, wo,
           tiles_per_seq_rows):
    (_, _, _, sza, _, _, _, szb, bc, cu, szc) = fr
    t_rows = resid.shape[0]
    tm = min(TM_MERGE, tiles_per_seq_rows)
    tiles_per_seq = tiles_per_seq_rows // tm
    n_tiles = t_rows // tm
    sub = tm // 8
    n_sub = t_rows // 8
    mod_row_fn = lambda i: mod_row_of_seq(i // tiles_per_seq)
    row = lambda cols: pl.BlockSpec((tm, cols), lambda i: (i, 0))
    const = lambda shape: pl.BlockSpec(shape, lambda i: (0,) * len(shape),
                                       pipeline_mode=pl.Buffered(1))
    in_specs = [
        row(512), row(512), row(512), row(512), row(512), row(512), row(512),
        pl.BlockSpec((8, 512), lambda i: (jnp.maximum(i * sub - 1, 0), 0)),
        pl.BlockSpec((8, 512), lambda i: (jnp.minimum((i + 1) * sub, n_sub - 1), 0)),
        row(1024),
        pl.BlockSpec((1, 1, D_MODEL), lambda i: (mod_row_fn(i), 0, 0)),
        pl.BlockSpec((1, 1, D_MODEL), lambda i: (mod_row_fn(i), 0, 1)),
        pl.BlockSpec((1, 1, D_MODEL), lambda i: (mod_row_fn(i), 0, 2)),
        const((1, D_MODEL)),
        const((1, D_MODEL)),
        const((3, W_C)),
        const((D_MODEL, 3 * D_MODEL)),
        const((3, W_BRANCH, D_MODEL)),
        const((D_MODEL, D_MODEL)),
    ]
    return pl.pallas_call(
        functools.partial(_merge_kernel, tiles_per_seq=tiles_per_seq),
        grid=(n_tiles,),
        in_specs=in_specs,
        out_specs=pl.BlockSpec((tm, D_MODEL), lambda i: (i, 0)),
        out_shape=jax.ShapeDtypeStruct((t_rows, D_MODEL), F32),
        compiler_params=pltpu.CompilerParams(
            dimension_semantics=("arbitrary",), vmem_limit_bytes=VMEM_LIMIT),
        name="merge",
    )(ya, yb, sza, szb, szc, bc, cu, cu, cu, resid, mod3, mod3, mod3, gpre, gpost, conv_w, wg,
      wbr, wo)


def _pack_w_in(w):
    o = [int(v) for v in np.concatenate([[0], np.cumsum(IN_SIZES)])]
    wb = w.astype(BF16)
    krp = jnp.pad(wb[:, o[6]:o[7]], ((0, 0), (NOPE_B, HEAD_PAD - NOPE_B - ROPE_B)))
    w2 = jnp.concatenate([wb[:, o[1]:o[2]], wb[:, o[3]:o[6]], krp, wb[:, o[7]:o[12]]], axis=1)
    wt = jnp.concatenate([wb[:, o[0]:o[1]], wb[:, o[2]:o[3]]], axis=1).T
    wg = wb[:, o[12]:o[15]]
    return w2, wt, wg


def _pack_w_qb(w):
    w3 = w.reshape(Q_LORA, H_B, NOPE_B + ROPE_B)
    w3 = jnp.pad(w3, ((0, 0), (0, 0), (0, HEAD_PAD - NOPE_B - ROPE_B)))
    return w3.reshape(Q_LORA, H_B * HEAD_PAD).T.astype(BF16)


def _pack_w_kvb(w):
    w3 = w.reshape(KV_LORA, H_B, NOPE_B + V_B)
    wkn = jnp.pad(w3[..., :NOPE_B], ((0, 0), (0, 0), (0, HEAD_PAD - NOPE_B)))
    wv = w3[..., NOPE_B:]
    return (wkn.reshape(KV_LORA, H_B * HEAD_PAD).astype(BF16),
            wv.reshape(KV_LORA, H_B * V_B).T.astype(BF16))


def _rope_tables(seq):
    t = jnp.arange(seq)
    row = (t // GRID_W).astype(F32)
    col = (t % GRID_W).astype(F32)

    def angles(rot_dim):
        axis_dim = rot_dim // 2
        inv = ROPE_BASE ** (-jnp.arange(0, axis_dim, 2, dtype=F32) / axis_dim)
        return jnp.concatenate([row[:, None] * inv, col[:, None] * inv], axis=-1)

    a = angles(HD_A)
    ca, sa = jnp.cos(a), jnp.sin(a)
    cos_a = jnp.tile(jnp.concatenate([ca, ca], axis=-1), (1, LANES // HD_A))
    sin_a = jnp.tile(jnp.concatenate([-sa, sa], axis=-1), (1, LANES // HD_A))
    b = angles(ROPE_B)
    cb, sb = jnp.cos(b), jnp.sin(b)
    tail = HEAD_PAD - NOPE_B - ROPE_B
    cos_b = jnp.concatenate([jnp.ones((seq, NOPE_B), F32), cb, cb, jnp.ones((seq, tail), F32)], axis=-1)
    sin_b = jnp.concatenate([jnp.zeros((seq, NOPE_B), F32), -sb, sb, jnp.zeros((seq, tail), F32)], axis=-1)
    return cos_a, sin_a, cos_b, sin_b, ca.T, sa.T, cb.T, sb.T


def kernel(x, c, ctx, c_ctx, w_mod, b_mod, g_pre, g_post, w_in, sink, g_qa, w_qb, g_kva, w_kvb,
           conv_w, w_branch, w_o):
    batch, seq, d = x.shape
    assert (seq, d) == (SEQ, D_MODEL) and ctx.shape == (batch, CTX_LEN, D_MODEL)
    assert batch <= 6
    depth = w_mod.shape[0]

    c_rows = jnp.concatenate([c, c_ctx[None, :], jnp.zeros((8 - batch - 1, d), F32)], axis=0)
    mod_all = _modulation(c_rows, w_mod, b_mod)
    tables = _rope_tables(seq)

    x2 = x.reshape(batch * seq, d)
    ctx2 = ctx.reshape(batch * CTX_LEN, d)
    x_tiles_front = seq // TM_FRONT

    for i in range(depth):
        mod3 = mod_all[i].reshape(8, 1, 3 * d)
        w2, wt, wg = _pack_w_in(w_in[i])
        wqbt = _pack_w_qb(w_qb[i])
        wkn, wvt = _pack_w_kvb(w_kvb[i])
        gpre = g_pre[i].reshape(1, d)
        gqa = g_qa[i].reshape(1, Q_LORA)
        gkva = g_kva[i].reshape(1, KV_LORA)
        gpost = g_post[i].reshape(1, d)
        wbr = w_branch[i].astype(BF16)
        wo = w_o[i].astype(BF16)

        fx = _front(x2, mod3, lambda t: t // x_tiles_front, gpre, w2, wt, gqa, wqbt, gkva, wkn,
                    wvt, tables, x_tiles_front)
        fc = _front(ctx2, mod3, lambda t: batch, gpre, w2, wt, gqa, wqbt, gkva, wkn, wvt, None, 1)
        qat, ka, vat, _, qbt, kb, vbt = fx[:7]
        qat_c, ka_c, vat_c, _, qbt_c, kb_c, vbt_c = fc[:7]

        ya = _window_attn(sink[i], qat, ka, vat, ka_c, vat_c, batch)
        yb = _mla_attn(qbt, kb, vbt, kb_c, vbt_c, batch)
        x2_new = _merge(ya, yb, fx, x2, mod3, lambda b: b, gpre, gpost,
                        conv_w[i], wg, wbr, wo, seq)
        if i < depth - 1:
            ya_c, yb_c = _ctx_attn(sink[i], qat_c, ka_c, vat_c, qbt_c, kb_c, vbt_c, batch)
            ctx2 = _merge(ya_c, yb_c, fc, ctx2, mod3, lambda b: batch, gpre, gpost,
                          conv_w[i], wg, wbr, wo, CTX_LEN)
        x2 = x2_new
    return x2.reshape(batch, seq, d)
```

```python
import functools

import jax
import jax.numpy as jnp
import numpy as np
from jax import lax
from jax.experimental import pallas as pl
from jax.experimental.pallas import tpu as pltpu

F32 = jnp.float32
BF16 = jnp.bfloat16

D_MODEL = 1024
SEQ = 8192
DEPTH = 2
CTX_LEN = 256
GRID_W = 64
WINDOW = 128
ROPE_BASE = 10000.0
EPS = 1e-6
NEG_INF = -1e30
LOG2E = float(np.log2(np.e))

H_A = 8
HKV_A = 2
G_A = H_A // HKV_A
HD_A = 64
QA_SCALE = HD_A ** -0.5 * LOG2E

H_B = 8
Q_LORA = 384
KV_LORA = 256
NOPE_B = 64
ROPE_B = 32
V_B = 64
QB_SCALE = (NOPE_B + ROPE_B) ** -0.5 * LOG2E

W_C = 512
W_BRANCH = 512

IN_SIZES = (H_A * HD_A, HKV_A * HD_A, HKV_A * HD_A, W_BRANCH,
            Q_LORA, KV_LORA, ROPE_B, W_BRANCH,
            W_C, W_C, W_C, W_BRANCH,
            D_MODEL, D_MODEL, D_MODEL)

LANES = 128
SUBLANES = 8
BF16_ROWS = 16
MOD_ROWS = SUBLANES
HEAD_PAD = 128

C_KA = 0
C_ZA = 128
C_QL = 640
C_KVL = 1024
C_KR = 1280
C_ZB = 1408
C_BC = 1920
C_CC = 2432
C_UC = 2944
C_ZC = 3456
C_END = 3968
R_QA = 0
R_VA = H_A * HD_A
R_END = R_VA + HKV_A * HD_A

VMEM_LIMIT = 56 * 1024 * 1024

TM_FRONT = 1024
TM_MERGE = 512
TQ_WIN = 256
TQ_MLA = 512
TK_MLA = 4096
KC_MLA = 512
SCORE_PAD = LANES


def _silu(v):
    return v * jax.nn.sigmoid(v)


def _rms(v, g):
    return v * lax.rsqrt(jnp.mean(v * v, axis=-1, keepdims=True) + EPS) * g


def _dot(a, b):
    return jnp.dot(a, b, preferred_element_type=F32)


def _dot_t(a, b):
    return lax.dot_general(a, b, (((1,), (1,)), ((), ())), preferred_element_type=F32)


def _mod_kernel(c_ref, w_ref, b_ref, o_ref):
    sc = _silu(c_ref[...])
    o_ref[0] = jnp.dot(sc, w_ref[0], preferred_element_type=F32,
                       precision=lax.Precision.HIGHEST) + b_ref[0]


def _modulation(c_rows, w_mod, b_mod):
    n_chunk = 3
    return pl.pallas_call(
        _mod_kernel,
        grid=(DEPTH, n_chunk),
        in_specs=[
            pl.BlockSpec((MOD_ROWS, D_MODEL), lambda l, j: (0, 0)),
            pl.BlockSpec((1, D_MODEL, D_MODEL), lambda l, j: (l, 0, j)),
            pl.BlockSpec((1, 1, D_MODEL), lambda l, j: (l, 0, j)),
        ],
        out_specs=pl.BlockSpec((1, MOD_ROWS, D_MODEL), lambda l, j: (l, 0, j)),
        out_shape=jax.ShapeDtypeStruct((DEPTH, MOD_ROWS, 3 * D_MODEL), F32),
        compiler_params=pltpu.CompilerParams(
            dimension_semantics=("arbitrary", "arbitrary"), vmem_limit_bytes=VMEM_LIMIT),
        name="modulation",
    )(c_rows, w_mod, b_mod.reshape(DEPTH, 1, 3 * D_MODEL))


def _rope_lanes(v, cos, sin, first_half, up_shift, dn_shift):
    rot = jnp.where(first_half, pltpu.roll(v, up_shift, 1), pltpu.roll(v, dn_shift, 1))
    return v * cos + rot * sin


def _rope_rows(x1, x2, cos, sin):
    return x1 * cos - x2 * sin, x2 * cos + x1 * sin


def _front_kernel(x_ref, shift_ref, scale_ref, gpre_ref, w_ref, wt_ref, gqa_ref, wqbt_ref,
                  gkva_ref, wkn_ref, wvt_ref, *rest, rope):
    if rope:
        (cosa_ref, sina_ref, cosb_ref, sinb_ref,
         cosat_ref, sinat_ref, cosbt_ref, sinbt_ref) = rest[:8]
        outs = rest[8:]
    else:
        outs = rest
    (qat_ref, ka_ref, vat_ref, sza_ref, qbt_ref, kb_ref, vbt_ref, szb_ref,
     bc_ref, cu_ref, szc_ref) = outs

    x = x_ref[...]
    h = _rms(x, gpre_ref[...]) * (1.0 + scale_ref[0]) + shift_ref[0]
    h = h.astype(BF16)

    def mm(a, b):
        return _dot(h, w_ref[:, a:b])

    qv_t = _dot_t(wt_ref[...], h)
    half = HD_A // 2
    for hh in range(H_A):
        r0 = R_QA + hh * HD_A
        x1, x2 = qv_t[r0:r0 + half], qv_t[r0 + half:r0 + HD_A]
        if rope:
            x1, x2 = _rope_rows(x1, x2, cosat_ref[...], sinat_ref[...])
        qat_ref[hh * HD_A:hh * HD_A + half, :] = (x1 * QA_SCALE).astype(BF16)
        qat_ref[hh * HD_A + half:(hh + 1) * HD_A, :] = (x2 * QA_SCALE).astype(BF16)
    vat_ref[...] = qv_t[R_VA:R_END].astype(BF16)
    ka = mm(C_KA, C_ZA)
    if rope:
        lane = lax.broadcasted_iota(jnp.int32, (1, LANES), 1)
        ka = _rope_lanes(ka, cosa_ref[...], sina_ref[...], (lane % HD_A) < half,
                         LANES - half, half)
    ka_ref[...] = ka.astype(BF16)
    sza_ref[...] = _silu(mm(C_ZA, C_QL)).astype(BF16)

    qn = _rms(mm(C_QL, C_KVL), gqa_ref[...]).astype(BF16)
    qb_t = _dot_t(wqbt_ref[...], qn)
    hr = ROPE_B // 2
    for hh in range(H_B):
        r0 = hh * HEAD_PAD
        x1 = qb_t[r0 + NOPE_B:r0 + NOPE_B + hr]
        x2 = qb_t[r0 + NOPE_B + hr:r0 + NOPE_B + ROPE_B]
        if rope:
            x1, x2 = _rope_rows(x1, x2, cosbt_ref[...], sinbt_ref[...])
        qbt_ref[r0:r0 + NOPE_B, :] = (qb_t[r0:r0 + NOPE_B] * QB_SCALE).astype(BF16)
        qbt_ref[r0 + NOPE_B:r0 + NOPE_B + hr, :] = (x1 * QB_SCALE).astype(BF16)
        qbt_ref[r0 + NOPE_B + hr:r0 + NOPE_B + ROPE_B, :] = (x2 * QB_SCALE).astype(BF16)
        qbt_ref[r0 + NOPE_B + ROPE_B:r0 + HEAD_PAD, :] = jnp.zeros(
            (HEAD_PAD - NOPE_B - ROPE_B, x.shape[0]), BF16)
    kvn = _rms(mm(C_KVL, C_KR), gkva_ref[...]).astype(BF16)
    kn = _dot(kvn, wkn_ref[...])
    kr = mm(C_KR, C_ZB)
    if rope:
        kr = _rope_lanes(kr, cosb_ref[...], sinb_ref[...], lane < (NOPE_B + hr),
                         LANES - hr, hr)
    for hh in range(H_B):
        sl = slice(hh * HEAD_PAD, (hh + 1) * HEAD_PAD)
        kb_ref[:, sl] = (kn[:, sl] + kr).astype(BF16)
    vbt_ref[...] = _dot_t(wvt_ref[...], kvn).astype(BF16)
    szb_ref[...] = _silu(mm(C_ZB, C_BC)).astype(BF16)

    bc_ref[...] = mm(C_BC, C_CC).astype(BF16)
    cu_ref[...] = (mm(C_CC, C_UC) * mm(C_UC, C_ZC)).astype(BF16)
    szc_ref[...] = _silu(mm(C_ZC, C_END)).astype(BF16)


_FRONT_OUTS = ((True, H_A * HD_A), (False, HKV_A * HD_A), (True, HKV_A * HD_A), (False, W_BRANCH),
               (True, H_B * HEAD_PAD), (False, H_B * HEAD_PAD), (True, H_B * V_B), (False, W_BRANCH),
               (False, W_C), (False, W_C), (False, W_BRANCH))


def _front(rows, mod3, mod_row_fn, gpre, w2, wt, gqa, wqbt, gkva, wkn, wvt, tables, tiles_per_seq):
    t_rows = rows.shape[0]
    tm = min(TM_FRONT, t_rows)
    n_tiles = t_rows // tm
    rope = tables is not None
    const = lambda shape: pl.BlockSpec(shape, lambda i: (0,) * len(shape),
                                       pipeline_mode=pl.Buffered(1))
    in_specs = [
        pl.BlockSpec((tm, D_MODEL), lambda i: (i, 0)),
        pl.BlockSpec((1, 1, D_MODEL), lambda i: (mod_row_fn(i), 0, 0)),
        pl.BlockSpec((1, 1, D_MODEL), lambda i: (mod_row_fn(i), 0, 1)),
        const((1, D_MODEL)),
        const((D_MODEL, C_END)),
        const((R_END, D_MODEL)),
        const((1, Q_LORA)),
        const((H_B * HEAD_PAD, Q_LORA)),
        const((1, KV_LORA)),
        const((KV_LORA, H_B * HEAD_PAD)),
        const((H_B * V_B, KV_LORA)),
    ]
    args = [rows, mod3, mod3, gpre, w2, wt, gqa, wqbt, gkva, wkn, wvt]
    if rope:
        pos = lambda i: i % tiles_per_seq
        in_specs += [pl.BlockSpec((tm, LANES), lambda i: (pos(i), 0))] * 4
        in_specs += [pl.BlockSpec((HD_A // 2, tm), lambda i: (0, pos(i)))] * 2
        in_specs += [pl.BlockSpec((ROPE_B // 2, tm), lambda i: (0, pos(i)))] * 2
        args += list(tables)
    out_specs, out_shape = [], []
    for transposed, width in _FRONT_OUTS:
        if transposed:
            out_specs.append(pl.BlockSpec((width, tm), lambda i: (0, i)))
            out_shape.append(jax.ShapeDtypeStruct((width, t_rows), BF16))
        else:
            out_specs.append(pl.BlockSpec((tm, width), lambda i: (i, 0)))
            out_shape.append(jax.ShapeDtypeStruct((t_rows, width), BF16))
    return pl.pallas_call(
        functools.partial(_front_kernel, rope=rope),
        grid=(n_tiles,),
        in_specs=in_specs,
        out_specs=out_specs,
        out_shape=out_shape,
        compiler_params=pltpu.CompilerParams(
            dimension_semantics=("arbitrary",), vmem_limit_bytes=VMEM_LIMIT),
        name="front_rope" if rope else "front_ctx",
    )(*args)


def _score_scratch(nk, tq):
    return pltpu.VMEM((2, nk, tq + SCORE_PAD), F32)


def _ones_rows(kc):
    return (lax.broadcasted_iota(jnp.int32, (BF16_ROWS, kc), 0) == 0).astype(BF16)


def _gqa_heads(sink_ref, qt_ref, chunks, bias_scr, s_scr, oT_scr):
    tq = qt_ref.shape[1]
    sizes = [k_ref.shape[0] for k_ref, _, _ in chunks]
    offs = [sum(sizes[:c]) for c in range(len(chunks))]
    n_chunks = len(chunks)
    n_pairs = H_A // 2
    zeros = jnp.zeros((HD_A, tq), BF16)
    ones_rows = {kc: _ones_rows(kc) for kc in set(sizes)}
    in_first = lax.broadcasted_iota(jnp.int32, (1, 2 * tq), 1) < tq

    def weights(j):
        cols = []
        for hh in (2 * j, 2 * j + 1):
            q_t = qt_ref[hh * HD_A:(hh + 1) * HD_A, :]
            cols.append(jnp.concatenate([q_t, zeros] if hh // G_A == 0 else [zeros, q_t], axis=0))
        return jnp.concatenate(cols, axis=1)

    def score_chunk(w, c, slot, mx):
        k_ref, _, masked = chunks[c]
        rows = slice(offs[c], offs[c] + sizes[c])
        s = _dot(k_ref[...], w)
        if masked:
            s = s + bias_scr[rows, :]
        s_scr[slot, rows, 0:2 * tq] = s
        cm = jnp.max(s, axis=0, keepdims=True)
        return cm if mx is None else jnp.maximum(mx, cm)

    mx = None
    w = weights(0)
    for c in range(n_chunks):
        mx = score_chunk(w, c, 0, mx)
    for j in range(n_pairs):
        g = (2 * j) // G_A
        slot = j % 2
        sink = jnp.where(in_first, sink_ref[2 * j] * LOG2E, sink_ref[2 * j + 1] * LOG2E)
        m = jnp.maximum(mx, sink)
        pv = None
        mx = None
        if j + 1 < n_pairs:
            w = weights(j + 1)
        for c in range(n_chunks):
            if j + 1 < n_pairs:
                mx = score_chunk(w, c, 1 - slot, mx)
            rows = slice(offs[c], offs[c] + sizes[c])
            p = jnp.exp2(s_scr[slot, rows, 0:2 * tq] - m).astype(BF16)
            vt = jnp.concatenate([chunks[c][1][g * HD_A:(g + 1) * HD_A, :], ones_rows[sizes[c]]],
                                 axis=0)
            d = _dot(vt, p)
            pv = d if pv is None else pv + d
        o_t = pv[:HD_A] / (pv[HD_A:HD_A + 1] + jnp.exp2(sink - m))
        oT_scr[2 * j * HD_A:(2 * j + 1) * HD_A, :] = o_t[:, 0:tq]
        oT_scr[(2 * j + 1) * HD_A:(2 * j + 2) * HD_A, :] = o_t[:, tq:2 * tq]


def _mla_pipeline(qt_ref, steps, m_scr, l_scr, acc_scr, s_scr):
    tq = qt_ref.shape[1]
    ones_rows = {size: _ones_rows(size) for _, chunks in steps for _, _, size in chunks}

    def offsets(chunks):
        return [sum(size for _, _, size in chunks[:c]) for c in range(len(chunks))]

    def score_chunk(step, c, slot, mx):
        hh, chunks = step
        k_fn, _, size = chunks[c]
        off = offsets(chunks)[c]
        s = _dot(k_fn(), qt_ref[hh * HEAD_PAD:(hh + 1) * HEAD_PAD, :])
        s_scr[slot, off:off + size, 0:tq] = s
        cm = jnp.max(s, axis=0, keepdims=True)
        return cm if mx is None else jnp.maximum(mx, cm)

    mx = None
    for c in range(len(steps[0][1])):
        mx = score_chunk(steps[0], c, 0, mx)
    for t, (hh, chunks) in enumerate(steps):
        slot = t % 2
        nxt = steps[t + 1] if t + 1 < len(steps) else None
        n_nxt = len(nxt[1]) if nxt is not None else 0
        offs = offsets(chunks)
        m_prev = m_scr[hh]
        m_new = jnp.maximum(m_prev, mx)
        alpha = jnp.exp2(m_prev - m_new)
        m_scr[hh] = m_new
        pv = None
        mx = None
        for c in range(max(len(chunks), n_nxt)):
            if c < n_nxt:
                mx = score_chunk(nxt, c, 1 - slot, mx)
            if c < len(chunks):
                _, vt_fn, size = chunks[c]
                p = jnp.exp2(s_scr[slot, offs[c]:offs[c] + size, 0:tq] - m_new).astype(BF16)
                d = _dot(jnp.concatenate([vt_fn(), ones_rows[size]], axis=0), p)
                pv = d if pv is None else pv + d
        l_scr[hh] = alpha * l_scr[hh] + pv[V_B:V_B + 1]
        rows = slice(hh * V_B, (hh + 1) * V_B)
        acc_scr[rows, :] = alpha * acc_scr[rows, :] + pv[:V_B]


def _mla_steps(k_ref, vt_ref, row0, sizes):
    def span(a, n):
        return slice(a, a + n) if isinstance(a, int) else pl.ds(pl.multiple_of(a, LANES), n)

    steps = []
    for hh in range(H_B):
        chunks, r = [], row0
        for size in sizes:
            chunks.append((
                functools.partial(lambda a, n, h: k_ref[span(a, n), h * HEAD_PAD:(h + 1) * HEAD_PAD],
                                  r, size, hh),
                functools.partial(lambda a, n, h: vt_ref[h * V_B:(h + 1) * V_B, span(a, n)],
                                  r, size, hh),
                size))
            r = r + size
        steps.append((hh, chunks))
    return steps


def _mla_init(m_scr, l_scr, acc_scr):
    m_scr[...] = jnp.full(m_scr.shape, NEG_INF, F32)
    l_scr[...] = jnp.zeros(l_scr.shape, F32)
    acc_scr[...] = jnp.zeros(acc_scr.shape, F32)


def _mla_finish(o_ref, l_scr, acc_scr):
    for hh in range(H_B):
        rows = slice(hh * V_B, (hh + 1) * V_B)
        acc_scr[rows, :] = acc_scr[rows, :] / l_scr[hh]
    o_ref[...] = acc_scr[...].T.astype(o_ref.dtype)


def _win_kernel(sink_ref, qt_ref, kp_ref, kc_ref, kn_ref, kx_ref, vp_ref, vc_ref, vn_ref, vx_ref,
                o_ref, oT_scr, s_scr, bias_scr):
    n = pl.program_id(1)
    tq = qt_ref.shape[1]
    n_loc = tq + 2 * WINDOW
    col = lax.broadcasted_iota(jnp.int32, (n_loc, 2 * tq), 1)
    col = jnp.where(col >= tq, col - tq, col)
    r = lax.broadcasted_iota(jnp.int32, (n_loc, 2 * tq), 0)
    kabs = n * tq - WINDOW + r
    ok = (jnp.abs(r - WINDOW - col) <= WINDOW) & (kabs >= 0) & (kabs < SEQ)
    bias_scr[...] = jnp.where(ok, 0.0, NEG_INF)
    chunks = [(kp_ref, vp_ref, True), (kc_ref, vc_ref, True), (kn_ref, vn_ref, True),
              (kx_ref, vx_ref, False)]
    _gqa_heads(sink_ref, qt_ref, chunks, bias_scr, s_scr, oT_scr)
    o_ref[...] = oT_scr[...].T.astype(o_ref.dtype)


def _window_attn(sink, qat, ka, vat, ka_c, vat_c, batch):
    tq = TQ_WIN
    nq = SEQ // tq
    sub = tq // WINDOW
    nw = SEQ // WINDOW
    kv_w = HKV_A * HD_A
    prev_i = lambda b, n: b * nw + jnp.maximum(n * sub - 1, 0)
    next_i = lambda b, n: b * nw + jnp.minimum((n + 1) * sub, nw - 1)
    grid_spec = pltpu.PrefetchScalarGridSpec(
        num_scalar_prefetch=1,
        grid=(batch, nq),
        in_specs=[
            pl.BlockSpec((H_A * HD_A, tq), lambda b, n, s: (0, b * nq + n)),
            pl.BlockSpec((WINDOW, kv_w), lambda b, n, s: (prev_i(b, n), 0)),
            pl.BlockSpec((tq, kv_w), lambda b, n, s: (b * nq + n, 0)),
            pl.BlockSpec((WINDOW, kv_w), lambda b, n, s: (next_i(b, n), 0)),
            pl.BlockSpec((CTX_LEN, kv_w), lambda b, n, s: (b, 0)),
            pl.BlockSpec((kv_w, WINDOW), lambda b, n, s: (0, prev_i(b, n))),
            pl.BlockSpec((kv_w, tq), lambda b, n, s: (0, b * nq + n)),
            pl.BlockSpec((kv_w, WINDOW), lambda b, n, s: (0, next_i(b, n))),
            pl.BlockSpec((kv_w, CTX_LEN), lambda b, n, s: (0, b)),
        ],
        out_specs=pl.BlockSpec((tq, H_A * HD_A), lambda b, n, s: (b * nq + n, 0)),
        scratch_shapes=[pltpu.VMEM((H_A * HD_A, tq), F32),
                        _score_scratch(tq + 2 * WINDOW + CTX_LEN, 2 * tq),
                        pltpu.VMEM((tq + 2 * WINDOW, 2 * tq), F32)],
    )
    return pl.pallas_call(
        _win_kernel,
        grid_spec=grid_spec,
        out_shape=jax.ShapeDtypeStruct((batch * SEQ, H_A * HD_A), BF16),
        compiler_params=pltpu.CompilerParams(
            dimension_semantics=("arbitrary", "arbitrary"), vmem_limit_bytes=VMEM_LIMIT),
        name="window_attn",
    )(sink, qat, ka, ka, ka, ka_c, vat, vat, vat, vat_c)


def _mla_kernel(qt_ref, kx_ref, vxt_ref, kc_ref, vct_ref, o_ref, m_scr, l_scr, acc_scr, s_scr):
    _mla_init(m_scr, l_scr, acc_scr)
    n_tiles = kx_ref.shape[0] // TK_MLA

    def body(j, carry):
        steps = _mla_steps(kx_ref, vxt_ref, j * TK_MLA, [KC_MLA] * (TK_MLA // KC_MLA))
        _mla_pipeline(qt_ref, steps, m_scr, l_scr, acc_scr, s_scr)
        return carry

    lax.fori_loop(0, n_tiles, body, 0)
    _mla_pipeline(qt_ref, _mla_steps(kc_ref, vct_ref, 0, [kc_ref.shape[0]]),
                  m_scr, l_scr, acc_scr, s_scr)
    _mla_finish(o_ref, l_scr, acc_scr)


def _mla_scratch(tq, tk):
    return [pltpu.VMEM((H_B, 1, tq), F32), pltpu.VMEM((H_B, 1, tq), F32),
            pltpu.VMEM((H_B * V_B, tq), F32), _score_scratch(tk, tq)]


def _mla_attn(qbt, kb, vbt, kb_c, vbt_c, batch):
    tq = TQ_MLA
    nq = SEQ // tq
    res = lambda shape, fn: pl.BlockSpec(shape, fn, pipeline_mode=pl.Buffered(1))
    return pl.pallas_call(
        _mla_kernel,
        grid=(batch, nq),
        in_specs=[
            pl.BlockSpec((H_B * HEAD_PAD, tq), lambda b, i: (0, b * nq + i)),
            res((SEQ, H_B * HEAD_PAD), lambda b, i: (b, 0)),
            res((H_B * V_B, SEQ), lambda b, i: (0, b)),
            res((CTX_LEN, H_B * HEAD_PAD), lambda b, i: (b, 0)),
            res((H_B * V_B, CTX_LEN), lambda b, i: (0, b)),
        ],
        out_specs=pl.BlockSpec((tq, H_B * V_B), lambda b, i: (b * nq + i, 0)),
        out_shape=jax.ShapeDtypeStruct((batch * SEQ, H_B * V_B), BF16),
        scratch_shapes=_mla_scratch(tq, TK_MLA),
        compiler_params=pltpu.CompilerParams(
            dimension_semantics=("arbitrary", "arbitrary"), vmem_limit_bytes=VMEM_LIMIT),
        name="mla_attn",
    )(qbt, kb, vbt, kb_c, vbt_c)


def _ctx_attn_kernel(sink_ref, qat_ref, ka_ref, vat_ref, qbt_ref, kb_ref, vbt_ref,
                     ya_ref, yb_ref, oT_scr, m_scr, l_scr, acc_scr, s_scr):
    _gqa_heads(sink_ref, qat_ref, [(ka_ref, vat_ref, False)], None, s_scr, oT_scr)
    ya_ref[...] = oT_scr[...].T.astype(ya_ref.dtype)
    _mla_init(m_scr, l_scr, acc_scr)
    _mla_pipeline(qbt_ref, _mla_steps(kb_ref, vbt_ref, 0, [kb_ref.shape[0]]),
                  m_scr, l_scr, acc_scr, s_scr)
    _mla_finish(yb_ref, l_scr, acc_scr)


def _ctx_attn(sink, qat_c, ka_c, vat_c, qbt_c, kb_c, vbt_c, batch):
    tok = lambda cols: pl.BlockSpec((CTX_LEN, cols), lambda b, s: (b, 0))
    feat = lambda rows: pl.BlockSpec((rows, CTX_LEN), lambda b, s: (0, b))
    grid_spec = pltpu.PrefetchScalarGridSpec(
        num_scalar_prefetch=1,
        grid=(batch,),
        in_specs=[feat(H_A * HD_A), tok(HKV_A * HD_A), feat(HKV_A * HD_A),
                  feat(H_B * HEAD_PAD), tok(H_B * HEAD_PAD), feat(H_B * V_B)],
        out_specs=[tok(H_A * HD_A), tok(H_B * V_B)],
        scratch_shapes=([pltpu.VMEM((H_A * HD_A, CTX_LEN), F32)] + _mla_scratch(CTX_LEN, CTX_LEN)[:3]
                        + [_score_scratch(CTX_LEN, 2 * CTX_LEN)]),
    )
    return pl.pallas_call(
        _ctx_attn_kernel,
        grid_spec=grid_spec,
        out_shape=[jax.ShapeDtypeStruct((batch * CTX_LEN, W_BRANCH), BF16)] * 2,
        compiler_params=pltpu.CompilerParams(
            dimension_semantics=("arbitrary",), vmem_limit_bytes=VMEM_LIMIT),
        name="ctx_attn",
    )(sink, qat_c, ka_c, vat_c, qbt_c, kb_c, vbt_c)


def _merge_kernel(ya_ref, yb_ref, sza_ref, szb_ref, szc_ref, bc_ref, cu_ref, cup_ref, cun_ref,
                  x_ref, shift_ref, scale_ref, gate_ref, gpre_ref, gpost_ref, cw_ref, wg_ref,
                  wbr_ref, wo_ref, o_ref, *, tiles_per_seq):
    i = pl.program_id(0)
    tm = x_ref.shape[0]
    x = x_ref[...]
    h = (_rms(x, gpre_ref[...]) * (1.0 + scale_ref[0]) + shift_ref[0]).astype(BF16)
    pos = i % tiles_per_seq
    z = cu_ref[...].astype(F32)
    prev_row = jnp.where(pos == 0, 0.0, cup_ref[SUBLANES - 1:SUBLANES, :].astype(F32))
    next_row = jnp.where(pos == tiles_per_seq - 1, 0.0, cun_ref[0:1, :].astype(F32))
    row = lax.broadcasted_iota(jnp.int32, z.shape, 0)
    z_m1 = jnp.where(row == 0, prev_row, pltpu.roll(z, 1, 0))
    z_p1 = jnp.where(row == tm - 1, next_row, pltpu.roll(z, tm - 1, 0))
    conv = cw_ref[0:1, :] * z_m1 + cw_ref[1:2, :] * z + cw_ref[2:3, :] * z_p1
    yc = bc_ref[...].astype(F32) * conv

    def branch(y, sz_ref, k):
        t = (y * sz_ref[...].astype(F32)).astype(BF16)
        sg = jax.nn.sigmoid(_dot(h, wg_ref[:, k * D_MODEL:(k + 1) * D_MODEL]))
        return sg * _dot(t, wbr_ref[k])

    m = (branch(ya_ref[...].astype(F32), sza_ref, 0)
         + branch(yb_ref[...].astype(F32), szb_ref, 1)
         + branch(yc, szc_ref, 2))
    out = _dot(m.astype(BF16), wo_ref[...])
    o_ref[...] = x + gate_ref[0] * _rms(out, gpost_ref[...])


def _merge(ya, yb, fr, resid, mod3, mod_row_of_seq, gpre, gpost, conv_w, wg, wbr, wo,
           tiles_per_seq_rows):
    (_, _, _, sza, _, _, _, szb, bc, cu, szc) = fr
    t_rows = resid.shape[0]
    tm = min(TM_MERGE, tiles_per_seq_rows)
    tiles_per_seq = tiles_per_seq_rows // tm
    n_tiles = t_rows // tm
    sub = tm // SUBLANES
    n_sub = t_rows // SUBLANES
    mod_row_fn = lambda i: mod_row_of_seq(i // tiles_per_seq)
    row = lambda cols: pl.BlockSpec((tm, cols), lambda i: (i, 0))
    const = lambda shape: pl.BlockSpec(shape, lambda i: (0,) * len(shape),
                                       pipeline_mode=pl.Buffered(1))
    in_specs = [
        row(W_BRANCH), row(W_BRANCH), row(W_BRANCH), row(W_BRANCH), row(W_BRANCH),
        row(W_C), row(W_C),
        pl.BlockSpec((SUBLANES, W_C), lambda i: (jnp.maximum(i * sub - 1, 0), 0)),
        pl.BlockSpec((SUBLANES, W_C), lambda i: (jnp.minimum((i + 1) * sub, n_sub - 1), 0)),
        row(D_MODEL),
        pl.BlockSpec((1, 1, D_MODEL), lambda i: (mod_row_fn(i), 0, 0)),
        pl.BlockSpec((1, 1, D_MODEL), lambda i: (mod_row_fn(i), 0, 1)),
        pl.BlockSpec((1, 1, D_MODEL), lambda i: (mod_row_fn(i), 0, 2)),
        const((1, D_MODEL)),
        const((1, D_MODEL)),
        const((3, W_C)),
        const((D_MODEL, 3 * D_MODEL)),
        const((3, W_BRANCH, D_MODEL)),
        const((D_MODEL, D_MODEL)),
    ]
    return pl.pallas_call(
        functools.partial(_merge_kernel, tiles_per_seq=tiles_per_seq),
        grid=(n_tiles,),
        in_specs=in_specs,
        out_specs=pl.BlockSpec((tm, D_MODEL), lambda i: (i, 0)),
        out_shape=jax.ShapeDtypeStruct((t_rows, D_MODEL), F32),
        compiler_params=pltpu.CompilerParams(
            dimension_semantics=("arbitrary",), vmem_limit_bytes=VMEM_LIMIT),
        name="merge",
    )(ya, yb, sza, szb, szc, bc, cu, cu, cu, resid, mod3, mod3, mod3, gpre, gpost, conv_w, wg,
      wbr, wo)


def _pack_w_in(w):
    o = [int(v) for v in np.concatenate([[0], np.cumsum(IN_SIZES)])]
    wb = w.astype(BF16)
    krp = jnp.pad(wb[:, o[6]:o[7]], ((0, 0), (NOPE_B, HEAD_PAD - NOPE_B - ROPE_B)))
    w2 = jnp.concatenate([wb[:, o[1]:o[2]], wb[:, o[3]:o[6]], krp, wb[:, o[7]:o[12]]], axis=1)
    wt = jnp.concatenate([wb[:, o[0]:o[1]], wb[:, o[2]:o[3]]], axis=1).T
    wg = wb[:, o[12]:o[15]]
    return w2, wt, wg


def _pack_w_qb(w):
    w3 = w.reshape(Q_LORA, H_B, NOPE_B + ROPE_B)
    w3 = jnp.pad(w3, ((0, 0), (0, 0), (0, HEAD_PAD - NOPE_B - ROPE_B)))
    return w3.reshape(Q_LORA, H_B * HEAD_PAD).T.astype(BF16)


def _pack_w_kvb(w):
    w3 = w.reshape(KV_LORA, H_B, NOPE_B + V_B)
    wkn = jnp.pad(w3[..., :NOPE_B], ((0, 0), (0, 0), (0, HEAD_PAD - NOPE_B)))
    wv = w3[..., NOPE_B:]
    return (wkn.reshape(KV_LORA, H_B * HEAD_PAD).astype(BF16),
            wv.reshape(KV_LORA, H_B * V_B).T.astype(BF16))


def _rope_tables(seq):
    t = jnp.arange(seq)
    row = (t // GRID_W).astype(F32)
    col = (t % GRID_W).astype(F32)

    def angles(rot_dim):
        axis_dim = rot_dim // 2
        inv = ROPE_BASE ** (-jnp.arange(0, axis_dim, 2, dtype=F32) / axis_dim)
        return jnp.concatenate([row[:, None] * inv, col[:, None] * inv], axis=-1)

    a = angles(HD_A)
    ca, sa = jnp.cos(a), jnp.sin(a)
    cos_a = jnp.tile(jnp.concatenate([ca, ca], axis=-1), (1, LANES // HD_A))
    sin_a = jnp.tile(jnp.concatenate([-sa, sa], axis=-1), (1, LANES // HD_A))
    b = angles(ROPE_B)
    cb, sb = jnp.cos(b), jnp.sin(b)
    tail = HEAD_PAD - NOPE_B - ROPE_B
    cos_b = jnp.concatenate([jnp.ones((seq, NOPE_B), F32), cb, cb, jnp.ones((seq, tail), F32)], axis=-1)
    sin_b = jnp.concatenate([jnp.zeros((seq, NOPE_B), F32), -sb, sb, jnp.zeros((seq, tail), F32)], axis=-1)
    return cos_a, sin_a, cos_b, sin_b, ca.T, sa.T, cb.T, sb.T


def kernel(x, c, ctx, c_ctx, w_mod, b_mod, g_pre, g_post, w_in, sink, g_qa, w_qb, g_kva, w_kvb,
           conv_w, w_branch, w_o):
    batch, seq, d = x.shape
    assert (seq, d) == (SEQ, D_MODEL) and ctx.shape == (batch, CTX_LEN, D_MODEL)
    assert batch + 1 <= MOD_ROWS
    depth = w_mod.shape[0]

    c_rows = jnp.concatenate([c, c_ctx[None, :], jnp.zeros((MOD_ROWS - batch - 1, d), F32)], axis=0)
    mod_all = _modulation(c_rows, w_mod, b_mod)
    tables = _rope_tables(seq)

    x2 = x.reshape(batch * seq, d)
    ctx2 = ctx.reshape(batch * CTX_LEN, d)
    x_tiles_front = seq // TM_FRONT

    for i in range(depth):
        mod3 = mod_all[i].reshape(MOD_ROWS, 1, 3 * d)
        w2, wt, wg = _pack_w_in(w_in[i])
        wqbt = _pack_w_qb(w_qb[i])
        wkn, wvt = _pack_w_kvb(w_kvb[i])
        gpre = g_pre[i].reshape(1, d)
        gqa = g_qa[i].reshape(1, Q_LORA)
        gkva = g_kva[i].reshape(1, KV_LORA)
        gpost = g_post[i].reshape(1, d)
        wbr = w_branch[i].astype(BF16)
        wo = w_o[i].astype(BF16)

        fx = _front(x2, mod3, lambda t: t // x_tiles_front, gpre, w2, wt, gqa, wqbt, gkva, wkn,
                    wvt, tables, x_tiles_front)
        fc = _front(ctx2, mod3, lambda t: batch, gpre, w2, wt, gqa, wqbt, gkva, wkn, wvt, None, 1)
        qat, ka, vat, _, qbt, kb, vbt = fx[:7]
        qat_c, ka_c, vat_c, _, qbt_c, kb_c, vbt_c = fc[:7]

        ya = _window_attn(sink[i], qat, ka, vat, ka_c, vat_c, batch)
        yb = _mla_attn(qbt, kb, vbt, kb_c, vbt_c, batch)
        x2_new = _merge(ya, yb, fx, x2, mod3, lambda b: b, gpre, gpost,
                        conv_w[i], wg, wbr, wo, seq)
        if i < depth - 1:
            ya_c, yb_c = _ctx_attn(sink[i], qat_c, ka_c, vat_c, qbt_c, kb_c, vbt_c, batch)
            ctx2 = _merge(ya_c, yb_c, fc, ctx2, mod3, lambda b: batch, gpre, gpost,
                          conv_w[i], wg, wbr, wo, CTX_LEN)
        x2 = x2_new
    return x2.reshape(batch, seq, d)
```

```python
import functools

import jax
import jax.numpy as jnp
import numpy as np
from jax import lax
from jax.experimental import pallas as pl
from jax.experimental.pallas import tpu as pltpu

F32 = jnp.float32
BF16 = jnp.bfloat16

D_MODEL = 1024
SEQ = 8192
DEPTH = 2
CTX_LEN = 256
GRID_W = 64
WINDOW = 128
ROPE_BASE = 10000.0
EPS = 1e-6
NEG_INF = -1e30
LOG2E = float(np.log2(np.e))

H_A = 8
HKV_A = 2
G_A = H_A // HKV_A
HD_A = 64
QA_SCALE = HD_A ** -0.5 * LOG2E

H_B = 8
Q_LORA = 384
KV_LORA = 256
NOPE_B = 64
ROPE_B = 32
V_B = 64
QB_SCALE = (NOPE_B + ROPE_B) ** -0.5 * LOG2E

W_C = 512
W_BRANCH = 512

IN_SIZES = (H_A * HD_A, HKV_A * HD_A, HKV_A * HD_A, W_BRANCH,
            Q_LORA, KV_LORA, ROPE_B, W_BRANCH,
            W_C, W_C, W_C, W_BRANCH,
            D_MODEL, D_MODEL, D_MODEL)

LANES = 128
SUBLANES = 8
BF16_ROWS = 16
MOD_ROWS = SUBLANES
HEAD_PAD = 128

C_KA = 0
C_ZA = 128
C_QL = 640
C_KVL = 1024
C_KR = 1280
C_ZB = 1408
C_BC = 1920
C_CC = 2432
C_UC = 2944
C_ZC = 3456
C_END = 3968
R_QA = 0
R_VA = H_A * HD_A
R_END = R_VA + HKV_A * HD_A

VMEM_LIMIT = 56 * 1024 * 1024

TM_FRONT = 1024
TM_MERGE = 512
TQ_WIN = 256
TQ_MLA = 512
TK_MLA = 4096
KC_MLA = 512
SCORE_PAD = LANES


def _silu(v):
    return v * jax.nn.sigmoid(v)


def _rms(v, g):
    return v * lax.rsqrt(jnp.mean(v * v, axis=-1, keepdims=True) + EPS) * g


def _dot(a, b):
    return jnp.dot(a, b, preferred_element_type=F32)


def _dot_t(a, b):
    return lax.dot_general(a, b, (((1,), (1,)), ((), ())), preferred_element_type=F32)


def _mod_kernel(c_ref, w_ref, b_ref, o_ref):
    sc = _silu(c_ref[...])
    o_ref[0] = jnp.dot(sc, w_ref[0], preferred_element_type=F32,
                       precision=lax.Precision.HIGHEST) + b_ref[0]


def _modulation(c_rows, w_mod, b_mod):
    n_chunk = 3
    return pl.pallas_call(
        _mod_kernel,
        grid=(DEPTH, n_chunk),
        in_specs=[
            pl.BlockSpec((MOD_ROWS, D_MODEL), lambda l, j: (0, 0)),
            pl.BlockSpec((1, D_MODEL, D_MODEL), lambda l, j: (l, 0, j)),
            pl.BlockSpec((1, 1, D_MODEL), lambda l, j: (l, 0, j)),
        ],
        out_specs=pl.BlockSpec((1, MOD_ROWS, D_MODEL), lambda l, j: (l, 0, j)),
        out_shape=jax.ShapeDtypeStruct((DEPTH, MOD_ROWS, 3 * D_MODEL), F32),
        compiler_params=pltpu.CompilerParams(
            dimension_semantics=("arbitrary", "arbitrary"), vmem_limit_bytes=VMEM_LIMIT),
        name="modulation",
    )(c_rows, w_mod, b_mod.reshape(DEPTH, 1, 3 * D_MODEL))


def _rope_lanes(v, cos, sin, first_half, up_shift, dn_shift):
    rot = jnp.where(first_half, pltpu.roll(v, up_shift, 1), pltpu.roll(v, dn_shift, 1))
    return v * cos + rot * sin


def _rope_rows(x1, x2, cos, sin):
    return x1 * cos - x2 * sin, x2 * cos + x1 * sin


def _front_kernel(x_ref, shift_ref, scale_ref, gpre_ref, w_ref, wt_ref, gqa_ref, wqbt_ref,
                  gkva_ref, wkn_ref, wvt_ref, *rest, rope):
    if rope:
        (cosa_ref, sina_ref, cosb_ref, sinb_ref,
         cosat_ref, sinat_ref, cosbt_ref, sinbt_ref) = rest[:8]
        outs = rest[8:]
    else:
        outs = rest
    (qat_ref, ka_ref, vat_ref, sza_ref, qbt_ref, kb_ref, vbt_ref, szb_ref,
     bc_ref, cu_ref, szc_ref) = outs

    x = x_ref[...]
    h = _rms(x, gpre_ref[...]) * (1.0 + scale_ref[0]) + shift_ref[0]
    h = h.astype(BF16)

    def mm(a, b):
        return _dot(h, w_ref[:, a:b])

    qv_t = _dot_t(wt_ref[...], h)
    half = HD_A // 2
    for hh in range(H_A):
        r0 = R_QA + hh * HD_A
        x1, x2 = qv_t[r0:r0 + half], qv_t[r0 + half:r0 + HD_A]
        if rope:
            x1, x2 = _rope_rows(x1, x2, cosat_ref[...], sinat_ref[...])
        qat_ref[hh * HD_A:hh * HD_A + half, :] = (x1 * QA_SCALE).astype(BF16)
        qat_ref[hh * HD_A + half:(hh + 1) * HD_A, :] = (x2 * QA_SCALE).astype(BF16)
    vat_ref[...] = qv_t[R_VA:R_END].astype(BF16)
    ka = mm(C_KA, C_ZA)
    if rope:
        lane = lax.broadcasted_iota(jnp.int32, (1, LANES), 1)
        ka = _rope_lanes(ka, cosa_ref[...], sina_ref[...], (lane % HD_A) < half,
                         LANES - half, half)
    ka_ref[...] = ka.astype(BF16)
    sza_ref[...] = _silu(mm(C_ZA, C_QL)).astype(BF16)

    qn = _rms(mm(C_QL, C_KVL), gqa_ref[...]).astype(BF16)
    qb_t = _dot_t(wqbt_ref[...], qn)
    hr = ROPE_B // 2
    for hh in range(H_B):
        r0 = hh * HEAD_PAD
        x1 = qb_t[r0 + NOPE_B:r0 + NOPE_B + hr]
        x2 = qb_t[r0 + NOPE_B + hr:r0 + NOPE_B + ROPE_B]
        if rope:
            x1, x2 = _rope_rows(x1, x2, cosbt_ref[...], sinbt_ref[...])
        qbt_ref[r0:r0 + NOPE_B, :] = (qb_t[r0:r0 + NOPE_B] * QB_SCALE).astype(BF16)
        qbt_ref[r0 + NOPE_B:r0 + NOPE_B + hr, :] = (x1 * QB_SCALE).astype(BF16)
        qbt_ref[r0 + NOPE_B + hr:r0 + NOPE_B + ROPE_B, :] = (x2 * QB_SCALE).astype(BF16)
        qbt_ref[r0 + NOPE_B + ROPE_B:r0 + HEAD_PAD, :] = jnp.zeros(
            (HEAD_PAD - NOPE_B - ROPE_B, x.shape[0]), BF16)
    kvn = _rms(mm(C_KVL, C_KR), gkva_ref[...]).astype(BF16)
    kn = _dot(kvn, wkn_ref[...])
    kr = mm(C_KR, C_ZB)
    if rope:
        kr = _rope_lanes(kr, cosb_ref[...], sinb_ref[...], lane < (NOPE_B + hr),
                         LANES - hr, hr)
    for hh in range(H_B):
        sl = slice(hh * HEAD_PAD, (hh + 1) * HEAD_PAD)
        kb_ref[:, sl] = (kn[:, sl] + kr).astype(BF16)
    vbt_ref[...] = _dot_t(wvt_ref[...], kvn).astype(BF16)
    szb_ref[...] = _silu(mm(C_ZB, C_BC)).astype(BF16)

    bc_ref[...] = mm(C_BC, C_CC).astype(BF16)
    cu_ref[...] = (mm(C_CC, C_UC) * mm(C_UC, C_ZC)).astype(BF16)
    szc_ref[...] = _silu(mm(C_ZC, C_END)).astype(BF16)


_FRONT_OUTS = ((True, H_A * HD_A), (False, HKV_A * HD_A), (True, HKV_A * HD_A), (False, W_BRANCH),
               (True, H_B * HEAD_PAD), (False, H_B * HEAD_PAD), (True, H_B * V_B), (False, W_BRANCH),
               (False, W_C), (False, W_C), (False, W_BRANCH))


def _front(rows, mod3, mod_row_fn, gpre, w2, wt, gqa, wqbt, gkva, wkn, wvt, tables, tiles_per_seq):
    t_rows = rows.shape[0]
    tm = min(TM_FRONT, t_rows)
    n_tiles = t_rows // tm
    rope = tables is not None
    const = lambda shape: pl.BlockSpec(shape, lambda i: (0,) * len(shape),
                                       pipeline_mode=pl.Buffered(1))
    in_specs = [
        pl.BlockSpec((tm, D_MODEL), lambda i: (i, 0)),
        pl.BlockSpec((1, 1, D_MODEL), lambda i: (mod_row_fn(i), 0, 0)),
        pl.BlockSpec((1, 1, D_MODEL), lambda i: (mod_row_fn(i), 0, 1)),
        const((1, D_MODEL)),
        const((D_MODEL, C_END)),
        const((R_END, D_MODEL)),
        const((1, Q_LORA)),
        const((H_B * HEAD_PAD, Q_LORA)),
        const((1, KV_LORA)),
        const((KV_LORA, H_B * HEAD_PAD)),
        const((H_B * V_B, KV_LORA)),
    ]
    args = [rows, mod3, mod3, gpre, w2, wt, gqa, wqbt, gkva, wkn, wvt]
    if rope:
        pos = lambda i: i % tiles_per_seq
        in_specs += [pl.BlockSpec((tm, LANES), lambda i: (pos(i), 0))] * 4
        in_specs += [pl.BlockSpec((HD_A // 2, tm), lambda i: (0, pos(i)))] * 2
        in_specs += [pl.BlockSpec((ROPE_B // 2, tm), lambda i: (0, pos(i)))] * 2
        args += list(tables)
    out_specs, out_shape = [], []
    for transposed, width in _FRONT_OUTS:
        if transposed:
            out_specs.append(pl.BlockSpec((width, tm), lambda i: (0, i)))
            out_shape.append(jax.ShapeDtypeStruct((width, t_rows), BF16))
        else:
            out_specs.append(pl.BlockSpec((tm, width), lambda i: (i, 0)))
            out_shape.append(jax.ShapeDtypeStruct((t_rows, width), BF16))
    return pl.pallas_call(
        functools.partial(_front_kernel, rope=rope),
        grid=(n_tiles,),
        in_specs=in_specs,
        out_specs=out_specs,
        out_shape=out_shape,
        compiler_params=pltpu.CompilerParams(
            dimension_semantics=("arbitrary",), vmem_limit_bytes=VMEM_LIMIT),
        name="front_rope" if rope else "front_ctx",
    )(*args)


def _score_scratch(nk, tq):
    return pltpu.VMEM((2, nk, tq + SCORE_PAD), F32)


def _ones_rows(kc):
    return (lax.broadcasted_iota(jnp.int32, (BF16_ROWS, kc), 0) == 0).astype(BF16)


def _gqa_heads(sink_ref, qt_ref, chunks, bias_scr, s_scr, oT_scr):
    tq = qt_ref.shape[1]
    hq = tq // 2
    sizes = [ch[0].shape[0] for ch in chunks]
    offs = [sum(sizes[:c]) for c in range(len(chunks))]
    n_chunks = len(chunks)
    n_pairs = H_A // 2
    zeros = jnp.zeros((HD_A, tq), BF16)
    ones_rows = {kc: _ones_rows(kc) for kc in set(sizes)}
    col = lax.broadcasted_iota(jnp.int32, (1, 2 * tq), 1)
    in_first = (col // hq) % 2 == 0
    spans = {"all": (0, 2 * tq), "lo": (0, tq), "hi": (tq, 2 * tq)}

    def weights(j):
        ws = []
        for hh in (2 * j, 2 * j + 1):
            q_t = qt_ref[hh * HD_A:(hh + 1) * HD_A, :]
            ws.append(jnp.concatenate([q_t, zeros] if hh // G_A == 0 else [zeros, q_t], axis=0))
        return jnp.concatenate([ws[0][:, :hq], ws[1][:, :hq], ws[0][:, hq:], ws[1][:, hq:]],
                               axis=1)

    def widen(v, lo, hi):
        parts = []
        if lo > 0:
            parts.append(jnp.full((1, lo), NEG_INF, F32))
        parts.append(v)
        if hi < 2 * tq:
            parts.append(jnp.full((1, 2 * tq - hi), NEG_INF, F32))
        return v if len(parts) == 1 else jnp.concatenate(parts, axis=1)

    def score_chunk(w, c, slot, mx):
        k_ref, _, masked, which = chunks[c]
        lo, hi = spans[which]
        rows = slice(offs[c], offs[c] + sizes[c])
        s = _dot(k_ref[...], w[:, lo:hi])
        if masked:
            s = s + bias_scr[rows, lo:hi]
        s_scr[slot, rows, lo:hi] = s
        cm = widen(jnp.max(s, axis=0, keepdims=True), lo, hi)
        return cm if mx is None else jnp.maximum(mx, cm)

    mx = None
    w = weights(0)
    for c in range(n_chunks):
        mx = score_chunk(w, c, 0, mx)
    for j in range(n_pairs):
        g = (2 * j) // G_A
        slot = j % 2
        sink = jnp.where(in_first, sink_ref[2 * j] * LOG2E, sink_ref[2 * j + 1] * LOG2E)
        m = jnp.maximum(mx, sink)
        pv = [None, None]
        mx = None
        if j + 1 < n_pairs:
            w = weights(j + 1)
        for c in range(n_chunks):
            if j + 1 < n_pairs:
                mx = score_chunk(w, c, 1 - slot, mx)
            lo, hi = spans[chunks[c][3]]
            rows = slice(offs[c], offs[c] + sizes[c])
            p = jnp.exp2(s_scr[slot, rows, lo:hi] - m[:, lo:hi]).astype(BF16)
            vt = jnp.concatenate([chunks[c][1][g * HD_A:(g + 1) * HD_A, :], ones_rows[sizes[c]]],
                                 axis=0)
            d = _dot(vt, p)
            for half in range(2):
                a, b = half * tq, (half + 1) * tq
                if lo <= a and b <= hi:
                    piece = d[:, a - lo:b - lo]
                    pv[half] = piece if pv[half] is None else pv[half] + piece
        pv = jnp.concatenate(pv, axis=1)
        o_t = pv[:HD_A] / (pv[HD_A:HD_A + 1] + jnp.exp2(sink - m))
        oT_scr[2 * j * HD_A:(2 * j + 1) * HD_A, :] = jnp.concatenate(
            [o_t[:, 0:hq], o_t[:, tq:tq + hq]], axis=1)
        oT_scr[(2 * j + 1) * HD_A:(2 * j + 2) * HD_A, :] = jnp.concatenate(
            [o_t[:, hq:tq], o_t[:, tq + hq:2 * tq]], axis=1)


def _mla_pipeline(qt_ref, steps, m_scr, l_scr, acc_scr, s_scr):
    tq = qt_ref.shape[1]
    ones_rows = {size: _ones_rows(size) for _, chunks in steps for _, _, size in chunks}

    def offsets(chunks):
        return [sum(size for _, _, size in chunks[:c]) for c in range(len(chunks))]

    def score_chunk(step, c, slot, mx):
        hh, chunks = step
        k_fn, _, size = chunks[c]
        off = offsets(chunks)[c]
        s = _dot(k_fn(), qt_ref[hh * HEAD_PAD:(hh + 1) * HEAD_PAD, :])
        s_scr[slot, off:off + size, 0:tq] = s
        cm = jnp.max(s, axis=0, keepdims=True)
        return cm if mx is None else jnp.maximum(mx, cm)

    mx = None
    for c in range(len(steps[0][1])):
        mx = score_chunk(steps[0], c, 0, mx)
    for t, (hh, chunks) in enumerate(steps):
        slot = t % 2
        nxt = steps[t + 1] if t + 1 < len(steps) else None
        n_nxt = len(nxt[1]) if nxt is not None else 0
        offs = offsets(chunks)
        m_prev = m_scr[hh]
        m_new = jnp.maximum(m_prev, mx)
        alpha = jnp.exp2(m_prev - m_new)
        m_scr[hh] = m_new
        pv = None
        mx = None
        for c in range(max(len(chunks), n_nxt)):
            if c < n_nxt:
                mx = score_chunk(nxt, c, 1 - slot, mx)
            if c < len(chunks):
                _, vt_fn, size = chunks[c]
                p = jnp.exp2(s_scr[slot, offs[c]:offs[c] + size, 0:tq] - m_new).astype(BF16)
                d = _dot(jnp.concatenate([vt_fn(), ones_rows[size]], axis=0), p)
                pv = d if pv is None else pv + d
        l_scr[hh] = alpha * l_scr[hh] + pv[V_B:V_B + 1]
        rows = slice(hh * V_B, (hh + 1) * V_B)
        acc_scr[rows, :] = alpha * acc_scr[rows, :] + pv[:V_B]


def _mla_steps(k_ref, vt_ref, row0, sizes):
    def span(a, n):
        return slice(a, a + n) if isinstance(a, int) else pl.ds(pl.multiple_of(a, LANES), n)

    steps = []
    for hh in range(H_B):
        chunks, r = [], row0
        for size in sizes:
            chunks.append((
                functools.partial(lambda a, n, h: k_ref[span(a, n), h * HEAD_PAD:(h + 1) * HEAD_PAD],
                                  r, size, hh),
                functools.partial(lambda a, n, h: vt_ref[h * V_B:(h + 1) * V_B, span(a, n)],
                                  r, size, hh),
                size))
            r = r + size
        steps.append((hh, chunks))
    return steps


def _mla_init(m_scr, l_scr, acc_scr):
    m_scr[...] = jnp.full(m_scr.shape, NEG_INF, F32)
    l_scr[...] = jnp.zeros(l_scr.shape, F32)
    acc_scr[...] = jnp.zeros(acc_scr.shape, F32)


def _mla_finish(o_ref, l_scr, acc_scr):
    for hh in range(H_B):
        rows = slice(hh * V_B, (hh + 1) * V_B)
        acc_scr[rows, :] = acc_scr[rows, :] / l_scr[hh]
    o_ref[...] = acc_scr[...].T.astype(o_ref.dtype)


def _win_kernel(sink_ref, qt_ref, kp_ref, kc_ref, kn_ref, kx_ref, vp_ref, vc_ref, vn_ref, vx_ref,
                o_ref, oT_scr, s_scr, bias_scr):
    n = pl.program_id(1)
    tq = qt_ref.shape[1]
    n_loc = tq + 2 * WINDOW
    hq = tq // 2
    col = lax.broadcasted_iota(jnp.int32, (n_loc, 2 * tq), 1)
    col = col % hq + jnp.where(col >= tq, hq, 0)
    r = lax.broadcasted_iota(jnp.int32, (n_loc, 2 * tq), 0)
    kabs = n * tq - WINDOW + r
    ok = (jnp.abs(r - WINDOW - col) <= WINDOW) & (kabs >= 0) & (kabs < SEQ)
    bias_scr[...] = jnp.where(ok, 0.0, NEG_INF)
    assert tq == 2 * WINDOW
    chunks = [(kp_ref, vp_ref, True, "lo"), (kc_ref, vc_ref, True, "all"),
              (kn_ref, vn_ref, True, "hi"), (kx_ref, vx_ref, False, "all")]
    _gqa_heads(sink_ref, qt_ref, chunks, bias_scr, s_scr, oT_scr)
    o_ref[...] = oT_scr[...].T.astype(o_ref.dtype)


def _window_attn(sink, qat, ka, vat, ka_c, vat_c, batch):
    tq = TQ_WIN
    nq = SEQ // tq
    sub = tq // WINDOW
    nw = SEQ // WINDOW
    kv_w = HKV_A * HD_A
    prev_i = lambda b, n: b * nw + jnp.maximum(n * sub - 1, 0)
    next_i = lambda b, n: b * nw + jnp.minimum((n + 1) * sub, nw - 1)
    grid_spec = pltpu.PrefetchScalarGridSpec(
        num_scalar_prefetch=1,
        grid=(batch, nq),
        in_specs=[
            pl.BlockSpec((H_A * HD_A, tq), lambda b, n, s: (0, b * nq + n)),
            pl.BlockSpec((WINDOW, kv_w), lambda b, n, s: (prev_i(b, n), 0)),
            pl.BlockSpec((tq, kv_w), lambda b, n, s: (b * nq + n, 0)),
            pl.BlockSpec((WINDOW, kv_w), lambda b, n, s: (next_i(b, n), 0)),
            pl.BlockSpec((CTX_LEN, kv_w), lambda b, n, s: (b, 0)),
            pl.BlockSpec((kv_w, WINDOW), lambda b, n, s: (0, prev_i(b, n))),
            pl.BlockSpec((kv_w, tq), lambda b, n, s: (0, b * nq + n)),
            pl.BlockSpec((kv_w, WINDOW), lambda b, n, s: (0, next_i(b, n))),
            pl.BlockSpec((kv_w, CTX_LEN), lambda b, n, s: (0, b)),
        ],
        out_specs=pl.BlockSpec((tq, H_A * HD_A), lambda b, n, s: (b * nq + n, 0)),
        scratch_shapes=[pltpu.VMEM((H_A * HD_A, tq), F32),
                        _score_scratch(tq + 2 * WINDOW + CTX_LEN, 2 * tq),
                        pltpu.VMEM((tq + 2 * WINDOW, 2 * tq), F32)],
    )
    return pl.pallas_call(
        _win_kernel,
        grid_spec=grid_spec,
        out_shape=jax.ShapeDtypeStruct((batch * SEQ, H_A * HD_A), BF16),
        compiler_params=pltpu.CompilerParams(
            dimension_semantics=("arbitrary", "arbitrary"), vmem_limit_bytes=VMEM_LIMIT),
        name="window_attn",
    )(sink, qat, ka, ka, ka, ka_c, vat, vat, vat, vat_c)


def _mla_kernel(qt_ref, kx_ref, vxt_ref, kc_ref, vct_ref, o_ref, m_scr, l_scr, acc_scr, s_scr):
    _mla_init(m_scr, l_scr, acc_scr)
    n_tiles = kx_ref.shape[0] // TK_MLA

    def body(j, carry):
        steps = _mla_steps(kx_ref, vxt_ref, j * TK_MLA, [KC_MLA] * (TK_MLA // KC_MLA))
        _mla_pipeline(qt_ref, steps, m_scr, l_scr, acc_scr, s_scr)
        return carry

    lax.fori_loop(0, n_tiles, body, 0)
    _mla_pipeline(qt_ref, _mla_steps(kc_ref, vct_ref, 0, [kc_ref.shape[0]]),
                  m_scr, l_scr, acc_scr, s_scr)
    _mla_finish(o_ref, l_scr, acc_scr)


def _mla_scratch(tq, tk):
    return [pltpu.VMEM((H_B, 1, tq), F32), pltpu.VMEM((H_B, 1, tq), F32),
            pltpu.VMEM((H_B * V_B, tq), F32), _score_scratch(tk, tq)]


def _mla_attn(qbt, kb, vbt, kb_c, vbt_c, batch):
    tq = TQ_MLA
    nq = SEQ // tq
    res = lambda shape, fn: pl.BlockSpec(shape, fn, pipeline_mode=pl.Buffered(1))
    return pl.pallas_call(
        _mla_kernel,
        grid=(batch, nq),
        in_specs=[
            pl.BlockSpec((H_B * HEAD_PAD, tq), lambda b, i: (0, b * nq + i)),
            res((SEQ, H_B * HEAD_PAD), lambda b, i: (b, 0)),
            res((H_B * V_B, SEQ), lambda b, i: (0, b)),
            res((CTX_LEN, H_B * HEAD_PAD), lambda b, i: (b, 0)),
            res((H_B * V_B, CTX_LEN), lambda b, i: (0, b)),
        ],
        out_specs=pl.BlockSpec((tq, H_B * V_B), lambda b, i: (b * nq + i, 0)),
        out_shape=jax.ShapeDtypeStruct((batch * SEQ, H_B * V_B), BF16),
        scratch_shapes=_mla_scratch(tq, TK_MLA),
        compiler_params=pltpu.CompilerParams(
            dimension_semantics=("arbitrary", "arbitrary"), vmem_limit_bytes=VMEM_LIMIT),
        name="mla_attn",
    )(qbt, kb, vbt, kb_c, vbt_c)


def _ctx_attn_kernel(sink_ref, qat_ref, ka_ref, vat_ref, qbt_ref, kb_ref, vbt_ref,
                     ya_ref, yb_ref, oT_scr, m_scr, l_scr, acc_scr, s_scr):
    _gqa_heads(sink_ref, qat_ref, [(ka_ref, vat_ref, False, "all")], None, s_scr, oT_scr)
    ya_ref[...] = oT_scr[...].T.astype(ya_ref.dtype)
    _mla_init(m_scr, l_scr, acc_scr)
    _mla_pipeline(qbt_ref, _mla_steps(kb_ref, vbt_ref, 0, [kb_ref.shape[0]]),
                  m_scr, l_scr, acc_scr, s_scr)
    _mla_finish(yb_ref, l_scr, acc_scr)


def _ctx_attn(sink, qat_c, ka_c, vat_c, qbt_c, kb_c, vbt_c, batch):
    tok = lambda cols: pl.BlockSpec((CTX_LEN, cols), lambda b, s: (b, 0))
    feat = lambda rows: pl.BlockSpec((rows, CTX_LEN), lambda b, s: (0, b))
    grid_spec = pltpu.PrefetchScalarGridSpec(
        num_scalar_prefetch=1,
        grid=(batch,),
        in_specs=[feat(H_A * HD_A), tok(HKV_A * HD_A), feat(HKV_A * HD_A),
                  feat(H_B * HEAD_PAD), tok(H_B * HEAD_PAD), feat(H_B * V_B)],
        out_specs=[tok(H_A * HD_A), tok(H_B * V_B)],
        scratch_shapes=([pltpu.VMEM((H_A * HD_A, CTX_LEN), F32)] + _mla_scratch(CTX_LEN, CTX_LEN)[:3]
                        + [_score_scratch(CTX_LEN, 2 * CTX_LEN)]),
    )
    return pl.pallas_call(
        _ctx_attn_kernel,
        grid_spec=grid_spec,
        out_shape=[jax.ShapeDtypeStruct((batch * CTX_LEN, W_BRANCH), BF16)] * 2,
        compiler_params=pltpu.CompilerParams(
            dimension_semantics=("arbitrary",), vmem_limit_bytes=VMEM_LIMIT),
        name="ctx_attn",
    )(sink, qat_c, ka_c, vat_c, qbt_c, kb_c, vbt_c)


def _merge_kernel(ya_ref, yb_ref, sza_ref, szb_ref, szc_ref, bc_ref, cu_ref, cup_ref, cun_ref,
                  x_ref, shift_ref, scale_ref, gate_ref, gpre_ref, gpost_ref, cw_ref, wg_ref,
                  wbr_ref, wo_ref, o_ref, *, tiles_per_seq):
    i = pl.program_id(0)
    tm = x_ref.shape[0]
    x = x_ref[...]
    h = (_rms(x, gpre_ref[...]) * (1.0 + scale_ref[0]) + shift_ref[0]).astype(BF16)
    pos = i % tiles_per_seq
    z = cu_ref[...].astype(F32)
    prev_row = jnp.where(pos == 0, 0.0, cup_ref[SUBLANES - 1:SUBLANES, :].astype(F32))
    next_row = jnp.where(pos == tiles_per_seq - 1, 0.0, cun_ref[0:1, :].astype(F32))
    row = lax.broadcasted_iota(jnp.int32, z.shape, 0)
    z_m1 = jnp.where(row == 0, prev_row, pltpu.roll(z, 1, 0))
    z_p1 = jnp.where(row == tm - 1, next_row, pltpu.roll(z, tm - 1, 0))
    conv = cw_ref[0:1, :] * z_m1 + cw_ref[1:2, :] * z + cw_ref[2:3, :] * z_p1
    yc = bc_ref[...].astype(F32) * conv

    def branch(y, sz_ref, k):
        t = (y * sz_ref[...].astype(F32)).astype(BF16)
        sg = jax.nn.sigmoid(_dot(h, wg_ref[:, k * D_MODEL:(k + 1) * D_MODEL]))
        return sg * _dot(t, wbr_ref[k])

    m = (branch(ya_ref[...].astype(F32), sza_ref, 0)
         + branch(yb_ref[...].astype(F32), szb_ref, 1)
         + branch(yc, szc_ref, 2))
    out = _dot(m.astype(BF16), wo_ref[...])
    o_ref[...] = x + gate_ref[0] * _rms(out, gpost_ref[...])


def _merge(ya, yb, fr, resid, mod3, mod_row_of_seq, gpre, gpost, conv_w, wg, wbr, wo,
           tiles_per_seq_rows):
    (_, _, _, sza, _, _, _, szb, bc, cu, szc) = fr
    t_rows = resid.shape[0]
    tm = min(TM_MERGE, tiles_per_seq_rows)
    tiles_per_seq = tiles_per_seq_rows // tm
    n_tiles = t_rows // tm
    sub = tm // SUBLANES
    n_sub = t_rows // SUBLANES
    mod_row_fn = lambda i: mod_row_of_seq(i // tiles_per_seq)
    row = lambda cols: pl.BlockSpec((tm, cols), lambda i: (i, 0))
    const = lambda shape: pl.BlockSpec(shape, lambda i: (0,) * len(shape),
                                       pipeline_mode=pl.Buffered(1))
    in_specs = [
        row(W_BRANCH), row(W_BRANCH), row(W_BRANCH), row(W_BRANCH), row(W_BRANCH),
        row(W_C), row(W_C),
        pl.BlockSpec((SUBLANES, W_C), lambda i: (jnp.maximum(i * sub - 1, 0), 0)),
        pl.BlockSpec((SUBLANES, W_C), lambda i: (jnp.minimum((i + 1) * sub, n_sub - 1), 0)),
        row(D_MODEL),
        pl.BlockSpec((1, 1, D_MODEL), lambda i: (mod_row_fn(i), 0, 0)),
        pl.BlockSpec((1, 1, D_MODEL), lambda i: (mod_row_fn(i), 0, 1)),
        pl.BlockSpec((1, 1, D_MODEL), lambda i: (mod_row_fn(i), 0, 2)),
        const((1, D_MODEL)),
        const((1, D_MODEL)),
        const((3, W_C)),
        const((D_MODEL, 3 * D_MODEL)),
        const((3, W_BRANCH, D_MODEL)),
        const((D_MODEL, D_MODEL)),
    ]
    return pl.pallas_call(
        functools.partial(_merge_kernel, tiles_per_seq=tiles_per_seq),
        grid=(n_tiles,),
        in_specs=in_specs,
        out_specs=pl.BlockSpec((tm, D_MODEL), lambda i: (i, 0)),
        out_shape=jax.ShapeDtypeStruct((t_rows, D_MODEL), F32),
        compiler_params=pltpu.CompilerParams(
            dimension_semantics=("arbitrary",), vmem_limit_bytes=VMEM_LIMIT),
        name="merge",
    )(ya, yb, sza, szb, szc, bc, cu, cu, cu, resid, mod3, mod3, mod3, gpre, gpost, conv_w, wg,
      wbr, wo)


def _pack_w_in(w):
    o = [int(v) for v in np.concatenate([[0], np.cumsum(IN_SIZES)])]
    wb = w.astype(BF16)
    krp = jnp.pad(wb[:, o[6]:o[7]], ((0, 0), (NOPE_B, HEAD_PAD - NOPE_B - ROPE_B)))
    w2 = jnp.concatenate([wb[:, o[1]:o[2]], wb[:, o[3]:o[6]], krp, wb[:, o[7]:o[12]]], axis=1)
    wt = jnp.concatenate([wb[:, o[0]:o[1]], wb[:, o[2]:o[3]]], axis=1).T
    wg = wb[:, o[12]:o[15]]
    return w2, wt, wg


def _pack_w_qb(w):
    w3 = w.reshape(Q_LORA, H_B, NOPE_B + ROPE_B)
    w3 = jnp.pad(w3, ((0, 0), (0, 0), (0, HEAD_PAD - NOPE_B - ROPE_B)))
    return w3.reshape(Q_LORA, H_B * HEAD_PAD).T.astype(BF16)


def _pack_w_kvb(w):
    w3 = w.reshape(KV_LORA, H_B, NOPE_B + V_B)
    wkn = jnp.pad(w3[..., :NOPE_B], ((0, 0), (0, 0), (0, HEAD_PAD - NOPE_B)))
    wv = w3[..., NOPE_B:]
    return (wkn.reshape(KV_LORA, H_B * HEAD_PAD).astype(BF16),
            wv.reshape(KV_LORA, H_B * V_B).T.astype(BF16))


def _rope_tables(seq):
    t = jnp.arange(seq)
    row = (t // GRID_W).astype(F32)
    col = (t % GRID_W).astype(F32)

    def angles(rot_dim):
        axis_dim = rot_dim // 2
        inv = ROPE_BASE ** (-jnp.arange(0, axis_dim, 2, dtype=F32) / axis_dim)
        return jnp.concatenate([row[:, None] * inv, col[:, None] * inv], axis=-1)

    a = angles(HD_A)
    ca, sa = jnp.cos(a), jnp.sin(a)
    cos_a = jnp.tile(jnp.concatenate([ca, ca], axis=-1), (1, LANES // HD_A))
    sin_a = jnp.tile(jnp.concatenate([-sa, sa], axis=-1), (1, LANES // HD_A))
    b = angles(ROPE_B)
    cb, sb = jnp.cos(b), jnp.sin(b)
    tail = HEAD_PAD - NOPE_B - ROPE_B
    cos_b = jnp.concatenate([jnp.ones((seq, NOPE_B), F32), cb, cb, jnp.ones((seq, tail), F32)], axis=-1)
    sin_b = jnp.concatenate([jnp.zeros((seq, NOPE_B), F32), -sb, sb, jnp.zeros((seq, tail), F32)], axis=-1)
    return cos_a, sin_a, cos_b, sin_b, ca.T, sa.T, cb.T, sb.T


def kernel(x, c, ctx, c_ctx, w_mod, b_mod, g_pre, g_post, w_in, sink, g_qa, w_qb, g_kva, w_kvb,
           conv_w, w_branch, w_o):
    batch, seq, d = x.shape
    assert (seq, d) == (SEQ, D_MODEL) and ctx.shape == (batch, CTX_LEN, D_MODEL)
    assert batch + 1 <= MOD_ROWS
    depth = w_mod.shape[0]

    c_rows = jnp.concatenate([c, c_ctx[None, :], jnp.zeros((MOD_ROWS - batch - 1, d), F32)], axis=0)
    mod_all = _modulation(c_rows, w_mod, b_mod)
    tables = _rope_tables(seq)

    x2 = x.reshape(batch * seq, d)
    ctx2 = ctx.reshape(batch * CTX_LEN, d)
    x_tiles_front = seq // TM_FRONT

    for i in range(depth):
        mod3 = mod_all[i].reshape(MOD_ROWS, 1, 3 * d)
        w2, wt, wg = _pack_w_in(w_in[i])
        wqbt = _pack_w_qb(w_qb[i])
        wkn, wvt = _pack_w_kvb(w_kvb[i])
        gpre = g_pre[i].reshape(1, d)
        gqa = g_qa[i].reshape(1, Q_LORA)
        gkva = g_kva[i].reshape(1, KV_LORA)
        gpost = g_post[i].reshape(1, d)
        wbr = w_branch[i].astype(BF16)
        wo = w_o[i].astype(BF16)

        fx = _front(x2, mod3, lambda t: t // x_tiles_front, gpre, w2, wt, gqa, wqbt, gkva, wkn,
                    wvt, tables, x_tiles_front)
        fc = _front(ctx2, mod3, lambda t: batch, gpre, w2, wt, gqa, wqbt, gkva, wkn, wvt, None, 1)
        qat, ka, vat, _, qbt, kb, vbt = fx[:7]
        qat_c, ka_c, vat_c, _, qbt_c, kb_c, vbt_c = fc[:7]

        ya = _window_attn(sink[i], qat, ka, vat, ka_c, vat_c, batch)
        yb = _mla_attn(qbt, kb, vbt, kb_c, vbt_c, batch)
        x2_new = _merge(ya, yb, fx, x2, mod3, lambda b: b, gpre, gpost,
                        conv_w[i], wg, wbr, wo, seq)
        if i < depth - 1:
            ya_c, yb_c = _ctx_attn(sink[i], qat_c, ka_c, vat_c, qbt_c, kb_c, vbt_c, batch)
            ctx2 = _merge(ya_c, yb_c, fc, ctx2, mod3, lambda b: batch, gpre, gpost,
                          conv_w[i], wg, wbr, wo, CTX_LEN)
        x2 = x2_new
    return x2.reshape(batch, seq, d)
```
